```python
import math
import jax, jax.numpy as jnp
from jax import lax
import numpy as np

D_MODEL = 2048
BATCH = 4
SEQ = 2048
DEPTH = 2
DEC_BATCH = 4
DEC_SEQ = 8192
PAST_LEN = 128

N_META = 16
GRID_W = 64
H_A = 4
HD_A = 128
W_A = H_A * 2 * HD_A
H_B = 8
HD_B = 128
W_B = H_B * HD_B
NA_WIN_R = 8
NA_WIN_C = 16
D_FF = 5632
CONV_W = 3
ROPE_THETA = 10000.0
Q_BLOCK = 128
EPS = 1e-6
SPLIT_SIZES = (W_A, W_A, W_A, W_B, W_B, W_B, D_MODEL, D_MODEL)
D_IN = 3 * W_A + 3 * W_B + 2 * D_MODEL

kernel_name = 'hybrid_diff_natten_encoder'


def rms_norm(x, g):
    x32 = x.astype(jnp.float32)
    y = x32 * lax.rsqrt(jnp.mean(x32 * x32, axis=-1, keepdims=True) + EPS)
    return (y * g.astype(jnp.float32)).astype(x.dtype)


def rope_tables(n_pos, dim):
    inv = 1.0 / (ROPE_THETA ** (jnp.arange(0, dim, 2, dtype=jnp.float32) / dim))
    ang = jnp.arange(n_pos, dtype=jnp.float32)[:, None] * inv[None, :]
    return jnp.cos(ang)[:, None, :], jnp.sin(ang)[:, None, :]


def apply_rope(x, cos, sin):
    half = x.shape[-1] // 2
    cos = cos.astype(x.dtype)
    sin = sin.astype(x.dtype)
    x1, x2 = x[..., :half], x[..., half:]
    return jnp.concatenate([x1 * cos - x2 * sin, x2 * cos + x1 * sin], axis=-1)


def lambda_init(layer):
    return 0.8 - 0.6 * math.exp(-0.3 * layer)


def diff_attention(q, k, v, lam):
    B, L = q.shape[0], q.shape[1]
    n_blocks = (L - N_META) // Q_BLOCK
    q = q * (HD_A ** -0.5)

    def attend(qb):
        s = jnp.einsum('bqhcd,bkhcd->bhcqk', qb, k).astype(jnp.float32)
        p = jax.nn.softmax(s, axis=-1)
        p = p[:, :, 0] - lam * p[:, :, 1]
        return jnp.einsum('bhqk,bkhe->bqhe', p.astype(v.dtype), v)

    o_meta = attend(q[:, :N_META])
    q_blocks = q[:, N_META:].reshape(B, n_blocks, Q_BLOCK, H_A, 2, HD_A).swapaxes(0, 1)
    o_real = lax.map(attend, q_blocks)
    o_real = o_real.swapaxes(0, 1).reshape(B, n_blocks * Q_BLOCK, H_A, 2 * HD_A)
    return jnp.concatenate([o_meta, o_real], axis=1)


def neighbourhood_attention(q, k, v, rpb):
    B, L = q.shape[0], q.shape[1]
    T = L - N_META
    rows = T // GRID_W
    wr = min(NA_WIN_R, rows)
    wc = NA_WIN_C
    q = q * (HD_B ** -0.5)
    qm, km, vm = q[:, :N_META], k[:, :N_META], v[:, :N_META]
    grid = (B, rows, GRID_W, H_B, HD_B)
    qg = q[:, N_META:].reshape(grid)
    kg = k[:, N_META:].reshape(grid)
    vg = v[:, N_META:].reshape(grid)
    row_start = np.clip(np.arange(rows) - wr // 2, 0, rows - wr).astype(np.int32)
    col_start = np.clip(np.arange(GRID_W) - wc // 2, 0, GRID_W - wc).astype(np.int32)
    col_idx = (col_start[:, None] + np.arange(wc)[None, :]).astype(np.int32)
    dr = (row_start[:, None] + np.arange(wr)[None, :] - np.arange(rows)[:, None] + (NA_WIN_R - 1)).astype(np.int32)
    dc = (col_idx - np.arange(GRID_W)[:, None] + (NA_WIN_C - 1)).astype(np.int32)
    bias = rpb[:, dr[:, None, :, None], dc[None, :, None, :]].astype(jnp.float32)
    bias = bias.transpose(1, 0, 2, 3, 4)

    def attend_row(args):
        q_r, r0, b_r = args
        k_rows = lax.dynamic_slice_in_dim(kg, r0, wr, axis=1)
        v_rows = lax.dynamic_slice_in_dim(vg, r0, wr, axis=1)
        k_win = k_rows[:, :, col_idx]
        v_win = v_rows[:, :, col_idx]
        s_win = jnp.einsum('bchd,bicjhd->bhcij', q_r, k_win).astype(jnp.float32) + b_r
        s_meta = jnp.einsum('bchd,bmhd->bhcm', q_r, km).astype(jnp.float32)
        s = jnp.concatenate([s_meta, s_win.reshape(B, H_B, GRID_W, wr * wc)], axis=-1)
        p = jax.nn.softmax(s, axis=-1).astype(v.dtype)
        p_meta = p[..., :N_META]
        p_win = p[..., N_META:].reshape(B, H_B, GRID_W, wr, wc)
        return (jnp.einsum('bhcm,bmhd->bchd', p_meta, vm)
                + jnp.einsum('bhcij,bicjhd->bchd', p_win, v_win))

    o_real = lax.map(attend_row, (qg.swapaxes(0, 1), jnp.asarray(row_start), bias))
    o_real = o_real.swapaxes(0, 1).reshape(B, T, H_B, HD_B)
    s_m = jnp.einsum('bqhd,bkhd->bhqk', qm, km).astype(jnp.float32)
    p_m = jax.nn.softmax(s_m, axis=-1).astype(v.dtype)
    o_meta = jnp.einsum('bhqk,bkhd->bqhd', p_m, vm)
    return jnp.concatenate([o_meta, o_real], axis=1)


def conv_ffn(h, w_up, cw, cb, w_down):
    gate, val = jnp.split(h @ w_up, 2, axis=-1)
    L = gate.shape[1]
    pad = CONV_W // 2
    gp = jnp.pad(gate, ((0, 0), (pad, pad), (0, 0)))
    acc = cb
    for j in range(CONV_W):
        acc = acc + gp[:, j:j + L] * cw[j]
    return (jax.nn.gelu(acc, approximate=True) * val) @ w_down


def encoder_layer(x, p, l, cos, sin):
    B, L, _ = x.shape
    f32 = jnp.float32
    h = rms_norm(x, p['norm_mix_pre'][l])
    splits = np.cumsum(SPLIT_SIZES)[:-1].tolist()
    q_a, k_a, v_a, q_b, k_b, v_b, g_a, g_b = jnp.split(h @ p['w_in'][l], splits, axis=-1)
    q_a = apply_rope(q_a.reshape(B, L, 2 * H_A, HD_A), cos, sin).reshape(B, L, H_A, 2, HD_A)
    k_a = apply_rope(k_a.reshape(B, L, 2 * H_A, HD_A), cos, sin).reshape(B, L, H_A, 2, HD_A)
    v_a = v_a.reshape(B, L, H_A, 2 * HD_A)
    lam0 = lambda_init(l)
    lam = (jnp.exp(jnp.sum(p['lam_q1'][l].astype(f32) * p['lam_k1'][l].astype(f32)))
           - jnp.exp(jnp.sum(p['lam_q2'][l].astype(f32) * p['lam_k2'][l].astype(f32))) + lam0)
    o_a = diff_attention(q_a, k_a, v_a, lam)
    o_a = rms_norm(o_a, p['subln'][l]) * (1.0 - lam0)
    o_b = neighbourhood_attention(q_b.reshape(B, L, H_B, HD_B), k_b.reshape(B, L, H_B, HD_B),
                                  v_b.reshape(B, L, H_B, HD_B), p['rpb'][l])
    mixed = (jax.nn.sigmoid(g_a) * (o_a.reshape(B, L, W_A) @ p['w_br_a'][l])
             + jax.nn.sigmoid(g_b) * (o_b.reshape(B, L, W_B) @ p['w_br_b'][l]))
    x = x + rms_norm(mixed @ p['w_out'][l], p['norm_mix_post'][l])
    h = rms_norm(x, p['norm_ffn_pre'][l])
    f = conv_ffn(h, p['w_ffn_up'][l], p['conv_w'][l], p['conv_b'][l], p['w_ffn_down'][l])
    return x + rms_norm(f, p['norm_ffn_post'][l])


def encode(x, p):
    B, T, _ = x.shape
    L = T + N_META
    meta = jnp.broadcast_to(p['meta_tokens'][None].astype(x.dtype), (B, N_META, D_MODEL))
    h = jnp.concatenate([meta, x], axis=1)
    cos, sin = rope_tables(L, HD_A)
    for l in range(DEPTH):
        h = encoder_layer(h, p, l, cos, sin)
    return h[:, N_META:]


def setup_inputs(seed: int = 0) -> dict:
    key = jax.random.key(seed)
    ks = jax.random.split(key, 24)

    def nrm(k, shape, scale):
        return jax.random.normal(k, shape, jnp.float32) * scale

    def gain(k, shape):
        return 1.0 + 0.05 * jax.random.normal(k, shape, jnp.float32)

    return {
        'x_prompt': nrm(ks[0], (BATCH, SEQ, D_MODEL), 1.0),
        'x_sample': nrm(ks[1], (DEC_BATCH, DEC_SEQ, D_MODEL), 1.0),
        'meta_tokens': nrm(ks[2], (N_META, D_MODEL), 1.0),
        'norm_mix_pre': gain(ks[3], (DEPTH, D_MODEL)),
        'w_in': nrm(ks[4], (DEPTH, D_MODEL, D_IN), D_MODEL ** -0.5),
        'lam_q1': nrm(ks[5], (DEPTH, HD_A), 0.1),
        'lam_k1': nrm(ks[6], (DEPTH, HD_A), 0.1),
        'lam_q2': nrm(ks[7], (DEPTH, HD_A), 0.1),
        'lam_k2': nrm(ks[8], (DEPTH, HD_A), 0.1),
        'subln': gain(ks[9], (DEPTH, 2 * HD_A)),
        'rpb': nrm(ks[10], (DEPTH, H_B, 2 * NA_WIN_R - 1, 2 * NA_WIN_C - 1), 0.05),
        'w_br_a': nrm(ks[11], (DEPTH, W_A, D_MODEL), W_A ** -0.5),
        'w_br_b': nrm(ks[12], (DEPTH, W_B, D_MODEL), W_B ** -0.5),
        'w_out': nrm(ks[13], (DEPTH, D_MODEL, D_MODEL), D_MODEL ** -0.5),
        'norm_mix_post': gain(ks[14], (DEPTH, D_MODEL)),
        'norm_ffn_pre': gain(ks[15], (DEPTH, D_MODEL)),
        'w_ffn_up': nrm(ks[16], (DEPTH, D_MODEL, 2 * D_FF), D_MODEL ** -0.5),
        'conv_w': nrm(ks[17], (DEPTH, CONV_W, D_FF), CONV_W ** -0.5),
        'conv_b': nrm(ks[18], (DEPTH, D_FF), 0.01),
        'w_ffn_down': nrm(ks[19], (DEPTH, D_FF, D_MODEL), D_FF ** -0.5),
        'norm_ffn_post': gain(ks[20], (DEPTH, D_MODEL)),
    }


def reference(x_prompt, x_sample, meta_tokens, norm_mix_pre, w_in, lam_q1, lam_k1, lam_q2, lam_k2,
              subln, rpb, w_br_a, w_br_b, w_out, norm_mix_post, norm_ffn_pre, w_ffn_up, conv_w,
              conv_b, w_ffn_down, norm_ffn_post):
    p = dict(meta_tokens=meta_tokens, norm_mix_pre=norm_mix_pre, w_in=w_in, lam_q1=lam_q1,
             lam_k1=lam_k1, lam_q2=lam_q2, lam_k2=lam_k2, subln=subln, rpb=rpb, w_br_a=w_br_a,
             w_br_b=w_br_b, w_out=w_out, norm_mix_post=norm_mix_post, norm_ffn_pre=norm_ffn_pre,
             w_ffn_up=w_ffn_up, conv_w=conv_w, conv_b=conv_b, w_ffn_down=w_ffn_down,
             norm_ffn_post=norm_ffn_post)
    y_prompt = encode(x_prompt, p)
    y_sample = encode(x_sample, p)
    return (y_prompt, y_sample)
```

```python
import functools
import math

import numpy as np
import jax
import jax.numpy as jnp
from jax import lax
from jax.experimental import pallas as pl
from jax.experimental.pallas import tpu as pltpu

D_MODEL = 2048
DEPTH = 2
N_META = 16
GRID_W = 64
H_A = 4
HD_A = 128
W_A = H_A * 2 * HD_A
H_B = 8
HD_B = 128
W_B = H_B * HD_B
NA_WIN_R = 8
NA_WIN_C = 16
D_FF = 5632
CONV_W = 3
ROPE_THETA = 10000.0
EPS = 1e-6

HEAD_ROWS = 128
PAD_ROWS = HEAD_ROWS - N_META
NEG_BIG = -1e30
VMEM_LIMIT = 56 * 1024 * 1024

PROJ_W = 2 * W_A + 2 * W_B + 2 * D_MODEL
PROJ_TILE = 1024
N_PROJ_TILES = PROJ_W // PROJ_TILE
N_VT_TILES = (W_A + W_B) // PROJ_TILE

NA_GROUP_ROWS = 4
NA_KEY_ROWS = 12

F32 = jnp.float32
BF16 = jnp.bfloat16
NT_DIMS = (((1,), (1,)), ((), ()))


def _params(*sem):
    return pltpu.CompilerParams(dimension_semantics=sem, vmem_limit_bytes=VMEM_LIMIT)


def _rms(x, gain):
    return x * lax.rsqrt(jnp.mean(x * x, axis=-1, keepdims=True) + EPS) * gain


def _pad_row_mask(tile, tm, lp, batch):
    r = tile * tm + lax.broadcasted_iota(jnp.int32, (tm, 1), 0)
    m = r < PAD_ROWS
    for b in range(1, batch):
        m = m | ((r >= b * lp) & (r < b * lp + PAD_ROWS))
    return m


def _proj_kernel(x_ref, g_ref, cos_ref, sin_ref, w_ref, wvt_ref, proj_ref, vt_ref, hn_ref):
    j = pl.program_id(1)

    @pl.when(j == 0)
    def _():
        hn_ref[...] = _rms(x_ref[...], g_ref[...]).astype(BF16)

    def main_dot():
        return jnp.dot(hn_ref[...], w_ref[...], preferred_element_type=F32)

    def rope_store(acc, scale):
        c = cos_ref[...]
        s = sin_ref[...]
        for grp in range(PROJ_TILE // HD_A):
            xg = acc[:, grp * HD_A:(grp + 1) * HD_A]
            r = xg * c + pltpu.roll(xg, HD_A // 2, 1) * s
            if scale is not None:
                r = r * scale
            proj_ref[:, grp * HD_A:(grp + 1) * HD_A] = r.astype(BF16)

    @pl.when(j == 0)
    def _():
        rope_store(main_dot(), HD_A ** -0.5)

    @pl.when(j == 1)
    def _():
        rope_store(main_dot(), None)

    @pl.when(j == 2)
    def _():
        proj_ref[...] = (main_dot() * (HD_B ** -0.5)).astype(BF16)

    @pl.when(j == 3)
    def _():
        proj_ref[...] = main_dot().astype(BF16)

    @pl.when((j >= 4) & (j < N_PROJ_TILES))
    def _():
        proj_ref[...] = jax.nn.sigmoid(main_dot()).astype(BF16)

    @pl.when(j >= N_PROJ_TILES)
    def _():
        vt_ref[...] = lax.dot_general(wvt_ref[...], hn_ref[...], NT_DIMS,
                                      preferred_element_type=F32).astype(BF16)


def _proj_call(x, gain, cos_t, sin_t, w_main, w_vt, tm):
    m = x.shape[0]
    last = N_PROJ_TILES - 1
    return pl.pallas_call(
        _proj_kernel,
        grid=(m // tm, N_PROJ_TILES + N_VT_TILES),
        in_specs=[
            pl.BlockSpec((tm, D_MODEL), lambda i, j: (i, 0)),
            pl.BlockSpec((1, D_MODEL), lambda i, j: (0, 0)),
            pl.BlockSpec((tm, HD_A), lambda i, j: (i, 0)),
            pl.BlockSpec((tm, HD_A), lambda i, j: (i, 0)),
            pl.BlockSpec((D_MODEL, PROJ_TILE), lambda i, j: (0, jnp.minimum(j, last))),
            pl.BlockSpec((PROJ_TILE, D_MODEL), lambda i, j: (jnp.maximum(j - N_PROJ_TILES, 0), 0)),
        ],
        out_specs=[
            pl.BlockSpec((tm, PROJ_TILE), lambda i, j: (i, jnp.minimum(j, last))),
            pl.BlockSpec((PROJ_TILE, tm), lambda i, j: (jnp.maximum(j - N_PROJ_TILES, 0), i)),
        ],
        out_shape=[
            jax.ShapeDtypeStruct((m, PROJ_W), BF16),
            jax.ShapeDtypeStruct((W_A + W_B, m), BF16),
        ],
        scratch_shapes=[pltpu.VMEM((tm, D_MODEL), BF16)],
        compiler_params=_params("parallel", "arbitrary"),
    )(x, gain, cos_t, sin_t, w_main, w_vt)


DA_TQ = 256
DA_TK = 512


def _dattn_kernel(q_ref, k_ref, vt_ref, lamv_ref, subln_ref, o_ref, m_ref, l_ref, acc_ref,
                  *, t_real, lam0):
    lv = lamv_ref[...]
    lam = (jnp.exp(jnp.sum(lv[0:1] * lv[1:2], axis=1, keepdims=True))
           - jnp.exp(jnp.sum(lv[2:3] * lv[3:4], axis=1, keepdims=True)) + lam0)
    subln = subln_ref[...]
    k_meta = k_ref[PAD_ROWS:HEAD_ROWS, :]
    vt_meta = vt_ref[:, PAD_ROWS:HEAD_ROWS]
    n_chunks = t_real // DA_TK

    def q_tile(q0, tq):
        q = q_ref[pl.ds(q0, tq), :]
        qc = [q[:, c * HD_A:(c + 1) * HD_A] for c in range(2)]

        for c in range(2):
            s = lax.dot_general(k_meta[:, c * HD_A:(c + 1) * HD_A], qc[c], NT_DIMS,
                                preferred_element_type=F32)
            m = jnp.max(s, axis=0, keepdims=True)
            p = jnp.exp(s - m)
            m_ref[c, :, 0:tq] = m
            l_ref[c, :, 0:tq] = jnp.sum(p, axis=0, keepdims=True)
            acc_ref[c, :, 0:tq] = jnp.dot(vt_meta, p.astype(BF16), preferred_element_type=F32)

        def chunk(j, carry):
            k0 = pl.multiple_of(HEAD_ROWS + j * DA_TK, 128)
            kk = k_ref[pl.ds(k0, DA_TK), :]
            vt = vt_ref[:, pl.ds(k0, DA_TK)]
            for c in range(2):
                s = lax.dot_general(kk[:, c * HD_A:(c + 1) * HD_A], qc[c], NT_DIMS,
                                    preferred_element_type=F32)
                m_old = m_ref[c, :, 0:tq]
                m_new = jnp.maximum(m_old, jnp.max(s, axis=0, keepdims=True))
                alpha = jnp.exp(m_old - m_new)
                p = jnp.exp(s - m_new)
                m_ref[c, :, 0:tq] = m_new
                l_ref[c, :, 0:tq] = alpha * l_ref[c, :, 0:tq] + jnp.sum(p, axis=0, keepdims=True)
                acc_ref[c, :, 0:tq] = (alpha * acc_ref[c, :, 0:tq]
                                       + jnp.dot(vt, p.astype(BF16), preferred_element_type=F32))
            return carry

        lax.fori_loop(0, n_chunks, chunk, 0)

        o0 = acc_ref[0, :, 0:tq] / l_ref[0, :, 0:tq]
        o1 = acc_ref[1, :, 0:tq] / l_ref[1, :, 0:tq]
        ot = o0 - lam * o1
        ms = jnp.mean(ot * ot, axis=0, keepdims=True)
        ot = ot * lax.rsqrt(ms + EPS) * subln * (1.0 - lam0)
        o_ref[pl.ds(q0, tq), :] = ot.T.astype(BF16)

    q_tile(0, HEAD_ROWS)

    def real_tile(i, carry):
        q_tile(pl.multiple_of(HEAD_ROWS + i * DA_TQ, 128), DA_TQ)
        return carry

    lax.fori_loop(0, t_real // DA_TQ, real_tile, 0)


def _dattn_call(proj3, vt, lamv, subln_col, layer, batch, lp):
    t_real = lp - HEAD_ROWS
    kern = functools.partial(_dattn_kernel, t_real=t_real,
                             lam0=0.8 - 0.6 * math.exp(-0.3 * layer))
    return pl.pallas_call(
        kern,
        grid=(batch, H_A),
        in_specs=[
            pl.BlockSpec((None, lp, 2 * HD_A), lambda b, h: (b, 0, h)),
            pl.BlockSpec((None, lp, 2 * HD_A), lambda b, h: (b, 0, H_A + h)),
            pl.BlockSpec((2 * HD_A, lp), lambda b, h: (h, b)),
            pl.BlockSpec((4, HD_A), lambda b, h: (0, 0)),
            pl.BlockSpec((2 * HD_A, 1), lambda b, h: (0, 0)),
        ],
        out_specs=pl.BlockSpec((None, lp, 2 * HD_A), lambda b, h: (b, 0, h)),
        out_shape=jax.ShapeDtypeStruct((batch, lp, W_A), BF16),
        scratch_shapes=[
            pltpu.VMEM((2, 1, DA_TQ), F32),
            pltpu.VMEM((2, 1, DA_TQ), F32),
            pltpu.VMEM((2, 2 * HD_A, DA_TQ), F32),
        ],
        compiler_params=_params("parallel", "parallel"),
    )(proj3, proj3, vt, lamv, subln_col)


NA_TQ = NA_GROUP_ROWS * GRID_W
NA_TK = NA_KEY_ROWS * GRID_W


def _na_kernel(q_ref, k_ref, vt_ref, bias_ref, o_ref, *, rows):
    n_groups = rows // NA_GROUP_ROWS
    k_meta = k_ref[PAD_ROWS:HEAD_ROWS, :]
    vt_meta = vt_ref[:, PAD_ROWS:HEAD_ROWS]

    qm = q_ref[0:HEAD_ROWS, :]
    s = lax.dot_general(k_meta, qm, NT_DIMS, preferred_element_type=F32)
    p = jnp.exp(s - jnp.max(s, axis=0, keepdims=True))
    ot = jnp.dot(vt_meta, p.astype(BF16), preferred_element_type=F32) / jnp.sum(p, axis=0, keepdims=True)
    o_ref[0:HEAD_ROWS, :] = ot.T.astype(BF16)

    def group(g, pattern, key_row):
        q0 = pl.multiple_of(HEAD_ROWS + g * NA_TQ, 128)
        k0 = pl.multiple_of(HEAD_ROWS + key_row * GRID_W, 128)
        q = q_ref[pl.ds(q0, NA_TQ), :]
        kw = k_ref[pl.ds(k0, NA_TK), :]
        vtw = vt_ref[:, pl.ds(k0, NA_TK)]
        s_win = lax.dot_general(kw, q, NT_DIMS, preferred_element_type=F32) + bias_ref[pattern]
        s_meta = lax.dot_general(k_meta, q, NT_DIMS, preferred_element_type=F32)
        m = jnp.maximum(jnp.max(s_win, axis=0, keepdims=True), jnp.max(s_meta, axis=0, keepdims=True))
        p_win = jnp.exp(s_win - m)
        p_meta = jnp.exp(s_meta - m)
        l = jnp.sum(p_win, axis=0, keepdims=True) + jnp.sum(p_meta, axis=0, keepdims=True)
        ot = (jnp.dot(vt_meta, p_meta.astype(BF16), preferred_element_type=F32)
              + jnp.dot(vtw, p_win.astype(BF16), preferred_element_type=F32)) / l
        o_ref[pl.ds(q0, NA_TQ), :] = ot.T.astype(BF16)

    group(0, 0, 0)

    def interior(g, carry):
        group(g, 1, g * NA_GROUP_ROWS - NA_WIN_R // 2)
        return carry

    lax.fori_loop(1, n_groups - 1, interior, 0)
    group(n_groups - 1, 2, rows - NA_KEY_ROWS)


def _na_call(proj3, vt, bias, batch, lp):
    rows = (lp - HEAD_ROWS) // GRID_W
    qb0 = (2 * W_A) // HD_B
    kb0 = (2 * W_A + W_B) // HD_B
    return pl.pallas_call(
        functools.partial(_na_kernel, rows=rows),
        grid=(H_B, batch),
        in_specs=[
            pl.BlockSpec((None, lp, HD_B), lambda h, b: (b, 0, qb0 + h)),
            pl.BlockSpec((None, lp, HD_B), lambda h, b: (b, 0, kb0 + h)),
            pl.BlockSpec((HD_B, lp), lambda h, b: (W_A // HD_B + h, b)),
            pl.BlockSpec((3, None, NA_TK, NA_TQ), lambda h, b: (0, h, 0, 0)),
        ],
        out_specs=pl.BlockSpec((None, lp, HD_B), lambda h, b: (b, 0, h)),
        out_shape=jax.ShapeDtypeStruct((batch, lp, W_B), BF16),
        compiler_params=_params("parallel", "parallel"),
    )(proj3, proj3, vt, bias)


def _na_bias_tables(rpb_l, rows):
    wr = min(NA_WIN_R, rows)
    last_r0 = rows - NA_GROUP_ROWS
    cases = [(0, 0), (2 * NA_GROUP_ROWS, NA_GROUP_ROWS), (last_r0, rows - NA_KEY_ROWS)]
    c = np.arange(GRID_W)
    col_start = np.clip(c - NA_WIN_C // 2, 0, GRID_W - NA_WIN_C)
    out = []
    for r0, kb in cases:
        r = r0 + np.arange(NA_GROUP_ROWS)
        row_start = np.clip(r - wr // 2, 0, rows - wr)
        kr = kb + np.arange(NA_KEY_ROWS)
        row_ok = (kr[:, None] >= row_start[None, :]) & (kr[:, None] < row_start[None, :] + wr)
        col_ok = (c[:, None] >= col_start[None, :]) & (c[:, None] < col_start[None, :] + NA_WIN_C)
        dr = np.clip(kr[:, None] - r[None, :] + (NA_WIN_R - 1), 0, 2 * NA_WIN_R - 2)
        dc = np.clip(c[:, None] - c[None, :] + (NA_WIN_C - 1), 0, 2 * NA_WIN_C - 2)
        ok = row_ok[:, None, :, None] & col_ok[None, :, None, :]
        dr4 = np.broadcast_to(dr[:, None, :, None], ok.shape).reshape(NA_TK, NA_TQ)
        dc4 = np.broadcast_to(dc[None, :, None, :], ok.shape).reshape(NA_TK, NA_TQ)
        vals = rpb_l[:, dr4, dc4].astype(F32)
        out.append(jnp.where(jnp.asarray(ok.reshape(NA_TK, NA_TQ))[None], vals, NEG_BIG))
    return jnp.stack(out, axis=0)


def _mix_kernel(oa_ref, ob_ref, ga_ref, gb_ref, x_ref, wa_ref, wb_ref, wo_ref, gain_ref, o_ref,
                *, tm, lp, batch):
    a = jnp.dot(oa_ref[...], wa_ref[...], preferred_element_type=F32)
    b = jnp.dot(ob_ref[...], wb_ref[...], preferred_element_type=F32)
    mixed = (ga_ref[...].astype(F32) * a + gb_ref[...].astype(F32) * b).astype(BF16)
    y = jnp.dot(mixed, wo_ref[...], preferred_element_type=F32)
    xn = x_ref[...] + _rms(y, gain_ref[...])
    o_ref[...] = jnp.where(_pad_row_mask(pl.program_id(0), tm, lp, batch), 0.0, xn)


def _mix_call(o_a, o_b, proj, x, w_a, w_b, w_o, gain, tm, lp, batch):
    m = x.shape[0]
    ga_blk = (2 * W_A + 2 * W_B) // D_MODEL
    resident = functools.partial(pl.BlockSpec, pipeline_mode=pl.Buffered(1))
    return pl.pallas_call(
        functools.partial(_mix_kernel, tm=tm, lp=lp, batch=batch),
        grid=(m // tm,),
        in_specs=[
            pl.BlockSpec((tm, W_A), lambda i: (i, 0)),
            pl.BlockSpec((tm, W_B), lambda i: (i, 0)),
            pl.BlockSpec((tm, D_MODEL), lambda i: (i, ga_blk)),
            pl.BlockSpec((tm, D_MODEL), lambda i: (i, ga_blk + 1)),
            pl.BlockSpec((tm, D_MODEL), lambda i: (i, 0)),
            resident((W_A, D_MODEL), lambda i: (0, 0)),
            resident((W_B, D_MODEL), lambda i: (0, 0)),
            resident((D_MODEL, D_MODEL), lambda i: (0, 0)),
            pl.BlockSpec((1, D_MODEL), lambda i: (0, 0)),
        ],
        out_specs=pl.BlockSpec((tm, D_MODEL), lambda i: (i, 0)),
        out_shape=jax.ShapeDtypeStruct((m, D_MODEL), F32),
        compiler_params=_params("parallel"),
    )(o_a, o_b, proj, proj, x, w_a, w_b, w_o, gain)


UP_TILE = 1024
DOWN_TK = 512
HALO = 16


def _up_kernel(x_ref, g_ref, w_ref, o_ref, hn_ref):
    @pl.when(pl.program_id(1) == 0)
    def _():
        hn_ref[...] = _rms(x_ref[...], g_ref[...]).astype(BF16)

    o_ref[...] = jnp.dot(hn_ref[...], w_ref[...], preferred_element_type=F32).astype(BF16)


def _up_call(x, gain, w_up, tm):
    m = x.shape[0]
    return pl.pallas_call(
        _up_kernel,
        grid=(m // tm, 2 * D_FF // UP_TILE),
        in_specs=[
            pl.BlockSpec((tm, D_MODEL), lambda i, j: (i, 0)),
            pl.BlockSpec((1, D_MODEL), lambda i, j: (0, 0)),
            pl.BlockSpec((D_MODEL, UP_TILE), lambda i, j: (0, j)),
        ],
        out_specs=pl.BlockSpec((tm, UP_TILE), lambda i, j: (i, j)),
        out_shape=jax.ShapeDtypeStruct((m, 2 * D_FF), BF16),
        scratch_shapes=[pltpu.VMEM((tm, D_MODEL), BF16)],
        compiler_params=_params("parallel", "arbitrary"),
    )(x, gain, w_up)


def _down_kernel(gate_ref, prev_ref, next_ref, val_ref, cw_ref, cb_ref, w_ref, x_ref, gain_ref,
                 o_ref, acc_ref, *, tm, n_tiles, lp, batch):
    i = pl.program_id(0)
    k = pl.program_id(1)
    g = gate_ref[...].astype(F32)
    prev_row = jnp.where(i == 0, 0.0, prev_ref[HALO - 1:HALO, :].astype(F32))
    next_row = jnp.where(i == n_tiles - 1, 0.0, next_ref[0:1, :].astype(F32))
    rid = lax.broadcasted_iota(jnp.int32, (tm, 1), 0)
    g_prev = jnp.where(rid == 0, prev_row, pltpu.roll(g, 1, 0))
    g_next = jnp.where(rid == tm - 1, next_row, pltpu.roll(g, tm - 1, 0))
    cw = cw_ref[...]
    a = cb_ref[...] + g_prev * cw[0:1] + g * cw[1:2] + g_next * cw[2:3]
    u = (jax.nn.gelu(a, approximate=True) * val_ref[...].astype(F32)).astype(BF16)
    contrib = jnp.dot(u, w_ref[...], preferred_element_type=F32)

    @pl.when(k == 0)
    def _():
        acc_ref[...] = contrib

    @pl.when(k > 0)
    def _():
        acc_ref[...] += contrib

    @pl.when(k == pl.num_programs(1) - 1)
    def _():
        xn = x_ref[...] + _rms(acc_ref[...], gain_ref[...])
        o_ref[...] = jnp.where(_pad_row_mask(i, tm, lp, batch), 0.0, xn)


def _down_call(up, x, cw, cb, w_down, gain, tm, lp, batch):
    m = x.shape[0]
    n_tiles = m // tm
    nk = D_FF // DOWN_TK
    hb = tm // HALO
    return pl.pallas_call(
        functools.partial(_down_kernel, tm=tm, n_tiles=n_tiles, lp=lp, batch=batch),
        grid=(n_tiles, nk),
        in_specs=[
            pl.BlockSpec((tm, DOWN_TK), lambda i, k: (i, k)),
            pl.BlockSpec((HALO, DOWN_TK), lambda i, k: (jnp.maximum(i * hb - 1, 0), k)),
            pl.BlockSpec((HALO, DOWN_TK), lambda i, k: (jnp.minimum((i + 1) * hb, m // HALO - 1), k)),
            pl.BlockSpec((tm, DOWN_TK), lambda i, k: (i, nk + k)),
            pl.BlockSpec((CONV_W, DOWN_TK), lambda i, k: (0, k)),
            pl.BlockSpec((1, DOWN_TK), lambda i, k: (0, k)),
            pl.BlockSpec((DOWN_TK, D_MODEL), lambda i, k: (k, 0)),
            pl.BlockSpec((tm, D_MODEL), lambda i, k: (i, 0)),
            pl.BlockSpec((1, D_MODEL), lambda i, k: (0, 0)),
        ],
        out_specs=pl.BlockSpec((tm, D_MODEL), lambda i, k: (i, 0)),
        out_shape=jax.ShapeDtypeStruct((m, D_MODEL), F32),
        scratch_shapes=[pltpu.VMEM((tm, D_MODEL), F32)],
        compiler_params=_params("parallel", "arbitrary"),
    )(up, up, up, up, cw, cb, w_down, x, gain)


def _rope_tables(lp, batch):
    pos = jnp.maximum(jnp.arange(lp, dtype=jnp.int32) - PAD_ROWS, 0).astype(F32)
    inv = 1.0 / (ROPE_THETA ** (jnp.arange(0, HD_A, 2, dtype=F32) / HD_A))
    ang = pos[:, None] * inv[None, :]
    cos, sin = jnp.cos(ang), jnp.sin(ang)
    cos_t = jnp.concatenate([cos, cos], axis=1)
    sin_t = jnp.concatenate([-sin, sin], axis=1)
    return jnp.tile(cos_t, (batch, 1)), jnp.tile(sin_t, (batch, 1))


def _layer_weights(l, w_in, w_br_a, w_br_b, w_out, w_ffn_up, w_ffn_down):
    wi = w_in[l]
    o = np.cumsum([0, W_A, W_A, W_A, W_B, W_B, W_B, D_MODEL, D_MODEL])
    seg = [wi[:, o[n]:o[n + 1]] for n in range(8)]
    w_main = jnp.concatenate([seg[0], seg[1], seg[3], seg[4], seg[6], seg[7]], axis=1).astype(BF16)
    w_vt = jnp.concatenate([seg[2], seg[5]], axis=1).T.astype(BF16)
    return dict(w_main=w_main, w_vt=w_vt, w_a=w_br_a[l].astype(BF16), w_b=w_br_b[l].astype(BF16),
                w_o=w_out[l].astype(BF16), w_up=w_ffn_up[l].astype(BF16),
                w_down=w_ffn_down[l].astype(BF16))


def _encode(x, meta_tokens, layers, p):
    batch, t_real, _ = x.shape
    lp = t_real + HEAD_ROWS
    m = batch * lp
    rows = t_real // GRID_W
    tm = 512
    tm_mix = 256
    head = jnp.concatenate([jnp.zeros((PAD_ROWS, D_MODEL), x.dtype), meta_tokens.astype(x.dtype)], axis=0)
    h = jnp.concatenate([jnp.broadcast_to(head[None], (batch, HEAD_ROWS, D_MODEL)), x], axis=1)
    h = h.reshape(m, D_MODEL)
    cos_t, sin_t = _rope_tables(lp, batch)
    for l in range(DEPTH):
        w = layers[l]
        row = lambda name: p[name][l].reshape(1, -1)
        proj, vt = _proj_call(h, row('norm_mix_pre'), cos_t, sin_t, w['w_main'], w['w_vt'], tm)
        proj3 = proj.reshape(batch, lp, PROJ_W)
        lamv = jnp.stack([p['lam_q1'][l], p['lam_k1'][l], p['lam_q2'][l], p['lam_k2'][l]]).astype(F32)
        o_a = _dattn_call(proj3, vt, lamv, p['subln'][l].reshape(-1, 1), l, batch, lp)
        o_b = _na_call(proj3, vt, _na_bias_tables(p['rpb'][l], rows), batch, lp)
        h = _mix_call(o_a.reshape(m, W_A), o_b.reshape(m, W_B), proj, h, w['w_a'], w['w_b'], w['w_o'],
                      row('norm_mix_post'), tm_mix, lp, batch)
        up = _up_call(h, row('norm_ffn_pre'), w['w_up'], tm)
        h = _down_call(up, h, p['conv_w'][l], row('conv_b'), w['w_down'], row('norm_ffn_post'),
                       tm, lp, batch)
    return h.reshape(batch, lp, D_MODEL)[:, HEAD_ROWS:]


def kernel(x_prompt, x_sample, meta_tokens, norm_mix_pre, w_in, lam_q1, lam_k1, lam_q2, lam_k2, subln, rpb, w_br_a, w_br_b, w_out, norm_mix_post, norm_ffn_pre, w_ffn_up, conv_w, conv_b, w_ffn_down, norm_ffn_post):
    p = dict(norm_mix_pre=norm_mix_pre, lam_q1=lam_q1, lam_k1=lam_k1, lam_q2=lam_q2, lam_k2=lam_k2,
             subln=subln, rpb=rpb, norm_mix_post=norm_mix_post, norm_ffn_pre=norm_ffn_pre,
             conv_w=conv_w, conv_b=conv_b, norm_ffn_post=norm_ffn_post)
    layers = [_layer_weights(l, w_in, w_br_a, w_br_b, w_out, w_ffn_up, w_ffn_down) for l in range(DEPTH)]
    y_prompt = _encode(x_prompt, meta_tokens, layers, p)
    y_sample = _encode(x_sample, meta_tokens, layers, p)
    return (y_prompt, y_sample)
```

```python
import functools
import math

import numpy as np
import jax
import jax.numpy as jnp
from jax import lax
from jax.experimental import pallas as pl
from jax.experimental.pallas import tpu as pltpu

D_MODEL = 2048
DEPTH = 2
N_META = 16
GRID_W = 64
H_A = 4
HD_A = 128
W_A = H_A * 2 * HD_A
H_B = 8
HD_B = 128
W_B = H_B * HD_B
NA_WIN_R = 8
NA_WIN_C = 16
D_FF = 5632
CONV_W = 3
ROPE_THETA = 10000.0
EPS = 1e-6

HEAD_ROWS = 128
PAD_ROWS = HEAD_ROWS - N_META
NEG_BIG = -1e30
LOG2_E = math.log2(math.e)
VMEM_LIMIT = 56 * 1024 * 1024

PROJ_W = 2 * W_A + 2 * W_B + 2 * D_MODEL
PROJ_TILE = 1024
N_PROJ_TILES = PROJ_W // PROJ_TILE
N_VT_TILES = (W_A + W_B) // PROJ_TILE

NA_GROUP_ROWS = 4
NA_KEY_ROWS = 12

F32 = jnp.float32
BF16 = jnp.bfloat16
NT_DIMS = (((1,), (1,)), ((), ()))


def _params(*sem):
    return pltpu.CompilerParams(dimension_semantics=sem, vmem_limit_bytes=VMEM_LIMIT)


def _rms(x, gain):
    return x * lax.rsqrt(jnp.mean(x * x, axis=-1, keepdims=True) + EPS) * gain


def _pad_row_mask(tile, tm, lp, batch):
    r = tile * tm + lax.broadcasted_iota(jnp.int32, (tm, 1), 0)
    m = r < PAD_ROWS
    for b in range(1, batch):
        m = m | ((r >= b * lp) & (r < b * lp + PAD_ROWS))
    return m


def _proj_kernel(x_ref, g_ref, cos_ref, sin_ref, w_ref, wvt_ref, proj_ref, vt_ref, hn_ref):
    j = pl.program_id(1)

    @pl.when(j == 0)
    def _():
        hn_ref[...] = _rms(x_ref[...], g_ref[...]).astype(BF16)

    def main_dot():
        return jnp.dot(hn_ref[...], w_ref[...], preferred_element_type=F32)

    def rope_store(acc, scale):
        c = cos_ref[...]
        s = sin_ref[...]
        for grp in range(PROJ_TILE // HD_A):
            xg = acc[:, grp * HD_A:(grp + 1) * HD_A]
            r = xg * c + pltpu.roll(xg, HD_A // 2, 1) * s
            if scale is not None:
                r = r * scale
            proj_ref[:, grp * HD_A:(grp + 1) * HD_A] = r.astype(BF16)

    @pl.when(j == 0)
    def _():
        rope_store(main_dot(), HD_A ** -0.5 * LOG2_E)

    @pl.when(j == 1)
    def _():
        rope_store(main_dot(), None)

    @pl.when(j == 2)
    def _():
        proj_ref[...] = (main_dot() * (HD_B ** -0.5 * LOG2_E)).astype(BF16)

    @pl.when(j == 3)
    def _():
        proj_ref[...] = main_dot().astype(BF16)

    @pl.when((j >= 4) & (j < N_PROJ_TILES))
    def _():
        proj_ref[...] = jax.nn.sigmoid(main_dot()).astype(BF16)

    @pl.when(j >= N_PROJ_TILES)
    def _():
        vt_ref[...] = lax.dot_general(wvt_ref[...], hn_ref[...], NT_DIMS,
                                      preferred_element_type=F32).astype(BF16)


def _proj_call(x, gain, cos_t, sin_t, w_main, w_vt, tm):
    m = x.shape[0]
    last = N_PROJ_TILES - 1
    return pl.pallas_call(
        _proj_kernel,
        grid=(m // tm, N_PROJ_TILES + N_VT_TILES),
        in_specs=[
            pl.BlockSpec((tm, D_MODEL), lambda i, j: (i, 0)),
            pl.BlockSpec((1, D_MODEL), lambda i, j: (0, 0)),
            pl.BlockSpec((tm, HD_A), lambda i, j: (i, 0)),
            pl.BlockSpec((tm, HD_A), lambda i, j: (i, 0)),
            pl.BlockSpec((D_MODEL, PROJ_TILE), lambda i, j: (0, jnp.minimum(j, last))),
            pl.BlockSpec((PROJ_TILE, D_MODEL), lambda i, j: (jnp.maximum(j - N_PROJ_TILES, 0), 0)),
        ],
        out_specs=[
            pl.BlockSpec((tm, PROJ_TILE), lambda i, j: (i, jnp.minimum(j, last))),
            pl.BlockSpec((PROJ_TILE, tm), lambda i, j: (jnp.maximum(j - N_PROJ_TILES, 0), i)),
        ],
        out_shape=[
            jax.ShapeDtypeStruct((m, PROJ_W), BF16),
            jax.ShapeDtypeStruct((W_A + W_B, m), BF16),
        ],
        scratch_shapes=[pltpu.VMEM((tm, D_MODEL), BF16)],
        compiler_params=_params("parallel", "arbitrary"),
        name="in_proj",
    )(x, gain, cos_t, sin_t, w_main, w_vt)


DA_TQ = 256
DA_TK = 512


def _dattn_kernel(q_ref, k_ref, vt_ref, lamv_ref, subln_ref, o_ref, acc0_ref, acc1_ref,
                  s00_ref, s01_ref, s10_ref, s11_ref, *, t_real, lam0):
    s_refs = ((s00_ref, s01_ref), (s10_ref, s11_ref))
    lv = lamv_ref[...]
    lam = (jnp.exp(jnp.sum(lv[0:1] * lv[1:2], axis=1, keepdims=True))
           - jnp.exp(jnp.sum(lv[2:3] * lv[3:4], axis=1, keepdims=True)) + lam0)
    subln = subln_ref[...]
    k_meta = k_ref[PAD_ROWS:HEAD_ROWS, :]
    vt_meta = vt_ref[:, PAD_ROWS:HEAD_ROWS]
    n_chunks = t_real // DA_TK
    acc_refs = (acc0_ref, acc1_ref)

    def q_tile(q0, tq):
        q = q_ref[pl.ds(q0, tq), :]
        qc = [q[:, c * HD_A:(c + 1) * HD_A] for c in range(2)]

        init = []
        for c in range(2):
            s = lax.dot_general(k_meta[:, c * HD_A:(c + 1) * HD_A], qc[c], NT_DIMS,
                                preferred_element_type=F32)
            m = jnp.max(s, axis=0, keepdims=True)
            p = jnp.exp2(s - m)
            acc_refs[c][:, 0:tq] = jnp.dot(vt_meta, p.astype(BF16), preferred_element_type=F32)
            init += [m, jnp.sum(p, axis=0, keepdims=True)]

        def chunk_start(j):
            return pl.multiple_of(HEAD_ROWS + j * DA_TK, 128)

        def scores(j, slot):
            kk = k_ref[pl.ds(chunk_start(j), DA_TK), :]
            for c in range(2):
                s_refs[slot][c][:, 0:tq] = lax.dot_general(
                    kk[:, c * HD_A:(c + 1) * HD_A], qc[c], NT_DIMS, preferred_element_type=F32)

        def accumulate(j, slot, carry):
            vt = vt_ref[:, pl.ds(chunk_start(j), DA_TK)]
            out = []
            for c in range(2):
                m_old, l_old = carry[2 * c], carry[2 * c + 1]
                s = s_refs[slot][c][:, 0:tq]
                m_new = jnp.maximum(m_old, jnp.max(s, axis=0, keepdims=True))
                alpha = jnp.exp2(m_old - m_new)
                p = jnp.exp2(s - m_new)
                acc_refs[c][:, 0:tq] = (alpha * acc_refs[c][:, 0:tq]
                                        + jnp.dot(vt, p.astype(BF16), preferred_element_type=F32))
                out += [m_new, alpha * l_old + jnp.sum(p, axis=0, keepdims=True)]
            return tuple(out)

        def chunk_pair(jj, carry):
            j = 2 * jj
            scores(j + 1, 1)
            carry = accumulate(j, 0, carry)
            scores(jnp.minimum(j + 2, n_chunks - 1), 0)
            return accumulate(j + 1, 1, carry)

        scores(0, 0)
        _, l0, _, l1 = lax.fori_loop(0, n_chunks // 2, chunk_pair, tuple(init))

        ot = acc0_ref[:, 0:tq] / l0 - lam * (acc1_ref[:, 0:tq] / l1)
        ms = jnp.mean(ot * ot, axis=0, keepdims=True)
        ot = ot * lax.rsqrt(ms + EPS) * subln * (1.0 - lam0)
        o_ref[pl.ds(q0, tq), :] = ot.T.astype(BF16)

    q_tile(0, HEAD_ROWS)

    def real_tile(i, carry):
        q_tile(pl.multiple_of(HEAD_ROWS + i * DA_TQ, 128), DA_TQ)
        return carry

    lax.fori_loop(0, t_real // DA_TQ, real_tile, 0)


def _dattn_call(proj3, vt, lamv, subln_col, layer, batch, lp):
    t_real = lp - HEAD_ROWS
    kern = functools.partial(_dattn_kernel, t_real=t_real,
                             lam0=0.8 - 0.6 * math.exp(-0.3 * layer))
    return pl.pallas_call(
        kern,
        grid=(batch, H_A),
        in_specs=[
            pl.BlockSpec((None, lp, 2 * HD_A), lambda b, h: (b, 0, h)),
            pl.BlockSpec((None, lp, 2 * HD_A), lambda b, h: (b, 0, H_A + h)),
            pl.BlockSpec((2 * HD_A, lp), lambda b, h: (h, b)),
            pl.BlockSpec((4, HD_A), lambda b, h: (0, 0)),
            pl.BlockSpec((2 * HD_A, 1), lambda b, h: (0, 0)),
        ],
        out_specs=pl.BlockSpec((None, lp, 2 * HD_A), lambda b, h: (b, 0, h)),
        out_shape=jax.ShapeDtypeStruct((batch, lp, W_A), BF16),
        scratch_shapes=[pltpu.VMEM((2 * HD_A, DA_TQ), F32)] * 2 + [pltpu.VMEM((DA_TK, DA_TQ), F32)] * 4,
        compiler_params=_params("parallel", "parallel"),
        name="diff_attn",
    )(proj3, proj3, vt, lamv, subln_col)


NA_TQ = NA_GROUP_ROWS * GRID_W
NA_TK = NA_KEY_ROWS * GRID_W


def _na_kernel(q_ref, k_ref, vt_ref, bias_ref, o_ref, *, rows):
    n_groups = rows // NA_GROUP_ROWS
    k_meta = k_ref[PAD_ROWS:HEAD_ROWS, :]
    vt_meta = vt_ref[:, PAD_ROWS:HEAD_ROWS]

    qm = q_ref[0:HEAD_ROWS, :]
    s = lax.dot_general(k_meta, qm, NT_DIMS, preferred_element_type=F32)
    p = jnp.exp2(s - jnp.max(s, axis=0, keepdims=True))
    ot = jnp.dot(vt_meta, p.astype(BF16), preferred_element_type=F32) / jnp.sum(p, axis=0, keepdims=True)
    o_ref[0:HEAD_ROWS, :] = ot.T.astype(BF16)

    def group(g, pattern, key_row):
        q0 = pl.multiple_of(HEAD_ROWS + g * NA_TQ, 128)
        k0 = pl.multiple_of(HEAD_ROWS + key_row * GRID_W, 128)
        q = q_ref[pl.ds(q0, NA_TQ), :]
        kw = k_ref[pl.ds(k0, NA_TK), :]
        vtw = vt_ref[:, pl.ds(k0, NA_TK)]
        s_win = lax.dot_general(kw, q, NT_DIMS, preferred_element_type=F32) + bias_ref[pattern]
        s_meta = lax.dot_general(k_meta, q, NT_DIMS, preferred_element_type=F32)
        m = jnp.maximum(jnp.max(s_win, axis=0, keepdims=True), jnp.max(s_meta, axis=0, keepdims=True))
        p_win = jnp.exp2(s_win - m)
        p_meta = jnp.exp2(s_meta - m)
        l = jnp.sum(p_win, axis=0, keepdims=True) + jnp.sum(p_meta, axis=0, keepdims=True)
        ot = (jnp.dot(vt_meta, p_meta.astype(BF16), preferred_element_type=F32)
              + jnp.dot(vtw, p_win.astype(BF16), preferred_element_type=F32)) / l
        o_ref[pl.ds(q0, NA_TQ), :] = ot.T.astype(BF16)

    group(0, 0, 0)

    def interior(g, carry):
        group(g, 1, g * NA_GROUP_ROWS - NA_WIN_R // 2)
        return carry

    lax.fori_loop(1, n_groups - 1, interior, 0)
    group(n_groups - 1, 2, rows - NA_KEY_ROWS)


def _na_call(proj3, vt, bias, batch, lp):
    rows = (lp - HEAD_ROWS) // GRID_W
    qb0 = (2 * W_A) // HD_B
    kb0 = (2 * W_A + W_B) // HD_B
    return pl.pallas_call(
        functools.partial(_na_kernel, rows=rows),
        grid=(H_B, batch),
        in_specs=[
            pl.BlockSpec((None, lp, HD_B), lambda h, b: (b, 0, qb0 + h)),
            pl.BlockSpec((None, lp, HD_B), lambda h, b: (b, 0, kb0 + h)),
            pl.BlockSpec((HD_B, lp), lambda h, b: (W_A // HD_B + h, b)),
            pl.BlockSpec((3, None, NA_TK, NA_TQ), lambda h, b: (0, h, 0, 0)),
        ],
        out_specs=pl.BlockSpec((None, lp, HD_B), lambda h, b: (b, 0, h)),
        out_shape=jax.ShapeDtypeStruct((batch, lp, W_B), BF16),
        compiler_params=_params("parallel", "parallel"),
        name="nbr_attn",
    )(proj3, proj3, vt, bias)


def _na_bias_tables(rpb_l, rows):
    wr = min(NA_WIN_R, rows)
    last_r0 = rows - NA_GROUP_ROWS
    cases = [(0, 0), (2 * NA_GROUP_ROWS, NA_GROUP_ROWS), (last_r0, rows - NA_KEY_ROWS)]
    n_dr = 2 * NA_WIN_R - 1
    n_dc = 2 * NA_WIN_C - 1
    toe = pl.pallas_call(
        _toeplitz_kernel,
        out_shape=jax.ShapeDtypeStruct((H_B * n_dr, GRID_W * GRID_W), F32),
        name="nbr_bias",
    )(rpb_l.reshape(H_B * n_dr, n_dc).astype(F32))
    toe = toe.reshape(H_B, n_dr, GRID_W, GRID_W)
    masked = jnp.full((H_B, GRID_W, GRID_W), NEG_BIG, F32)
    out = []
    for r0, kb in cases:
        slab = []
        for u in range(NA_KEY_ROWS):
            per_q = []
            for j in range(NA_GROUP_ROWS):
                r, kr = r0 + j, kb + u
                row_start = min(max(r - wr // 2, 0), rows - wr)
                inside = row_start <= kr < row_start + wr
                per_q.append(toe[:, kr - r + NA_WIN_R - 1] if inside else masked)
            slab.append(jnp.stack(per_q, axis=2))
        out.append(jnp.stack(slab, axis=1).reshape(H_B, NA_TK, NA_TQ))
    return jnp.stack(out, axis=0)


def _toeplitz_kernel(rpb_ref, o_ref):
    n = lax.broadcasted_iota(jnp.int32, (1, GRID_W * GRID_W), 1)
    kc = n >> (GRID_W.bit_length() - 1)
    c = n & (GRID_W - 1)
    dc = kc - c + (NA_WIN_C - 1)
    col_start = jnp.clip(c - NA_WIN_C // 2, 0, GRID_W - NA_WIN_C)
    inside = (kc >= col_start) & (kc < col_start + NA_WIN_C)
    acc = jnp.full(o_ref.shape, NEG_BIG, F32)
    for d in range(2 * NA_WIN_C - 1):
        acc = jnp.where(inside & (dc == d), rpb_ref[:, d:d + 1] * LOG2_E, acc)
    o_ref[...] = acc


def _mix_kernel(oa_ref, ob_ref, ga_ref, gb_ref, x_ref, wa_ref, wb_ref, wo_ref, gain_ref, o_ref,
                *, tm, lp, batch):
    a = jnp.dot(oa_ref[...], wa_ref[...], preferred_element_type=F32)
    b = jnp.dot(ob_ref[...], wb_ref[...], preferred_element_type=F32)
    mixed = (ga_ref[...].astype(F32) * a + gb_ref[...].astype(F32) * b).astype(BF16)
    y = jnp.dot(mixed, wo_ref[...], preferred_element_type=F32)
    xn = x_ref[...] + _rms(y, gain_ref[...])
    o_ref[...] = jnp.where(_pad_row_mask(pl.program_id(0), tm, lp, batch), 0.0, xn)


def _mix_call(o_a, o_b, proj, x, w_a, w_b, w_o, gain, tm, lp, batch):
    m = x.shape[0]
    ga_blk = (2 * W_A + 2 * W_B) // D_MODEL
    resident = functools.partial(pl.BlockSpec, pipeline_mode=pl.Buffered(1))
    return pl.pallas_call(
        functools.partial(_mix_kernel, tm=tm, lp=lp, batch=batch),
        grid=(m // tm,),
        in_specs=[
            pl.BlockSpec((tm, W_A), lambda i: (i, 0)),
            pl.BlockSpec((tm, W_B), lambda i: (i, 0)),
            pl.BlockSpec((tm, D_MODEL), lambda i: (i, ga_blk)),
            pl.BlockSpec((tm, D_MODEL), lambda i: (i, ga_blk + 1)),
            pl.BlockSpec((tm, D_MODEL), lambda i: (i, 0)),
            resident((W_A, D_MODEL), lambda i: (0, 0)),
            resident((W_B, D_MODEL), lambda i: (0, 0)),
            resident((D_MODEL, D_MODEL), lambda i: (0, 0)),
            pl.BlockSpec((1, D_MODEL), lambda i: (0, 0)),
        ],
        out_specs=pl.BlockSpec((tm, D_MODEL), lambda i: (i, 0)),
        out_shape=jax.ShapeDtypeStruct((m, D_MODEL), F32),
        compiler_params=_params("parallel"),
        name="mix",
    )(o_a, o_b, proj, proj, x, w_a, w_b, w_o, gain)


UP_TILE = 1024
DOWN_TK = 512
HALO = 16


def _up_kernel(x_ref, g_ref, w_ref, o_ref, hn_ref):
    @pl.when(pl.program_id(1) == 0)
    def _():
        hn_ref[...] = _rms(x_ref[...], g_ref[...]).astype(BF16)

    o_ref[...] = jnp.dot(hn_ref[...], w_ref[...], preferred_element_type=F32).astype(BF16)


def _up_call(x, gain, w_up, tm):
    m = x.shape[0]
    return pl.pallas_call(
        _up_kernel,
        grid=(m // tm, 2 * D_FF // UP_TILE),
        in_specs=[
            pl.BlockSpec((tm, D_MODEL), lambda i, j: (i, 0)),
            pl.BlockSpec((1, D_MODEL), lambda i, j: (0, 0)),
            pl.BlockSpec((D_MODEL, UP_TILE), lambda i, j: (0, j)),
        ],
        out_specs=pl.BlockSpec((tm, UP_TILE), lambda i, j: (i, j)),
        out_shape=jax.ShapeDtypeStruct((m, 2 * D_FF), BF16),
        scratch_shapes=[pltpu.VMEM((tm, D_MODEL), BF16)],
        compiler_params=_params("parallel", "arbitrary"),
        name="ffn_up",
    )(x, gain, w_up)


def _down_kernel(gate_ref, prev_ref, next_ref, val_ref, cw_ref, cb_ref, w_ref, x_ref, gain_ref,
                 o_ref, acc_ref, *, tm, n_tiles, lp, batch):
    i = pl.program_id(0)
    k = pl.program_id(1)
    g = gate_ref[...].astype(F32)
    prev_row = jnp.where(i == 0, 0.0, prev_ref[HALO - 1:HALO, :].astype(F32))
    next_row = jnp.where(i == n_tiles - 1, 0.0, next_ref[0:1, :].astype(F32))
    rid = lax.broadcasted_iota(jnp.int32, (tm, 1), 0)
    g_prev = jnp.where(rid == 0, prev_row, pltpu.roll(g, 1, 0))
    g_next = jnp.where(rid == tm - 1, next_row, pltpu.roll(g, tm - 1, 0))
    cw = cw_ref[...]
    a = cb_ref[...] + g_prev * cw[0:1] + g * cw[1:2] + g_next * cw[2:3]
    u = (jax.nn.gelu(a, approximate=True) * val_ref[...].astype(F32)).astype(BF16)
    contrib = jnp.dot(u, w_ref[...], preferred_element_type=F32)

    @pl.when(k == 0)
    def _():
        acc_ref[...] = contrib

    @pl.when(k > 0)
    def _():
        acc_ref[...] += contrib

    @pl.when(k == pl.num_programs(1) - 1)
    def _():
        xn = x_ref[...] + _rms(acc_ref[...], gain_ref[...])
        o_ref[...] = jnp.where(_pad_row_mask(i, tm, lp, batch), 0.0, xn)


def _down_call(up, x, cw, cb, w_down, gain, tm, lp, batch):
    m = x.shape[0]
    n_tiles = m // tm
    nk = D_FF // DOWN_TK
    hb = tm // HALO
    return pl.pallas_call(
        functools.partial(_down_kernel, tm=tm, n_tiles=n_tiles, lp=lp, batch=batch),
        grid=(n_tiles, nk),
        in_specs=[
            pl.BlockSpec((tm, DOWN_TK), lambda i, k: (i, k)),
            pl.BlockSpec((HALO, DOWN_TK), lambda i, k: (jnp.maximum(i * hb - 1, 0), k)),
            pl.BlockSpec((HALO, DOWN_TK), lambda i, k: (jnp.minimum((i + 1) * hb, m // HALO - 1), k)),
            pl.BlockSpec((tm, DOWN_TK), lambda i, k: (i, nk + k)),
            pl.BlockSpec((CONV_W, DOWN_TK), lambda i, k: (0, k)),
            pl.BlockSpec((1, DOWN_TK), lambda i, k: (0, k)),
            pl.BlockSpec((DOWN_TK, D_MODEL), lambda i, k: (k, 0)),
            pl.BlockSpec((tm, D_MODEL), lambda i, k: (i, 0)),
            pl.BlockSpec((1, D_MODEL), lambda i, k: (0, 0)),
        ],
        out_specs=pl.BlockSpec((tm, D_MODEL), lambda i, k: (i, 0)),
        out_shape=jax.ShapeDtypeStruct((m, D_MODEL), F32),
        scratch_shapes=[pltpu.VMEM((tm, D_MODEL), F32)],
        compiler_params=_params("parallel", "arbitrary"),
        name="ffn_down",
    )(up, up, up, up, cw, cb, w_down, x, gain)


def _rope_tables(lp, batch):
    pos = jnp.maximum(jnp.arange(lp, dtype=jnp.int32) - PAD_ROWS, 0).astype(F32)
    inv = 1.0 / (ROPE_THETA ** (jnp.arange(0, HD_A, 2, dtype=F32) / HD_A))
    ang = pos[:, None] * inv[None, :]
    cos, sin = jnp.cos(ang), jnp.sin(ang)
    cos_t = jnp.concatenate([cos, cos], axis=1)
    sin_t = jnp.concatenate([-sin, sin], axis=1)
    return jnp.tile(cos_t, (batch, 1)), jnp.tile(sin_t, (batch, 1))


def _layer_weights(l, w_in, w_br_a, w_br_b, w_out, w_ffn_up, w_ffn_down):
    wi = w_in[l]
    o = np.cumsum([0, W_A, W_A, W_A, W_B, W_B, W_B, D_MODEL, D_MODEL])
    seg = [wi[:, o[n]:o[n + 1]] for n in range(8)]
    w_main = jnp.concatenate([seg[0], seg[1], seg[3], seg[4], seg[6], seg[7]], axis=1).astype(BF16)
    w_vt = jnp.concatenate([seg[2], seg[5]], axis=1).T.astype(BF16)
    return dict(w_main=w_main, w_vt=w_vt, w_a=w_br_a[l].astype(BF16), w_b=w_br_b[l].astype(BF16),
                w_o=w_out[l].astype(BF16), w_up=w_ffn_up[l].astype(BF16),
                w_down=w_ffn_down[l].astype(BF16))


def _encode(x, meta_tokens, layers, p):
    batch, t_real, _ = x.shape
    lp = t_real + HEAD_ROWS
    m = batch * lp
    rows = t_real // GRID_W
    tm = 512
    tm_mix = 256
    head = jnp.concatenate([jnp.zeros((PAD_ROWS, D_MODEL), x.dtype), meta_tokens.astype(x.dtype)], axis=0)
    h = jnp.concatenate([jnp.broadcast_to(head[None], (batch, HEAD_ROWS, D_MODEL)), x], axis=1)
    h = h.reshape(m, D_MODEL)
    cos_t, sin_t = _rope_tables(lp, batch)
    for l in range(DEPTH):
        w = layers[l]
        row = lambda name: p[name][l].reshape(1, -1)
        proj, vt = _proj_call(h, row('norm_mix_pre'), cos_t, sin_t, w['w_main'], w['w_vt'], tm)
        proj3 = proj.reshape(batch, lp, PROJ_W)
        lamv = jnp.stack([p['lam_q1'][l], p['lam_k1'][l], p['lam_q2'][l], p['lam_k2'][l]]).astype(F32)
        o_a = _dattn_call(proj3, vt, lamv, p['subln'][l].reshape(-1, 1), l, batch, lp)
        o_b = _na_call(proj3, vt, _na_bias_tables(p['rpb'][l], rows), batch, lp)
        h = _mix_call(o_a.reshape(m, W_A), o_b.reshape(m, W_B), proj, h, w['w_a'], w['w_b'], w['w_o'],
                      row('norm_mix_post'), tm_mix, lp, batch)
        up = _up_call(h, row('norm_ffn_pre'), w['w_up'], tm)
        h = _down_call(up, h, p['conv_w'][l], row('conv_b'), w['w_down'], row('norm_ffn_post'),
                       tm, lp, batch)
    return h.reshape(batch, lp, D_MODEL)[:, HEAD_ROWS:]


def kernel(x_prompt, x_sample, meta_tokens, norm_mix_pre, w_in, lam_q1, lam_k1, lam_q2, lam_k2, subln, rpb, w_br_a, w_br_b, w_out, norm_mix_post, norm_ffn_pre, w_ffn_up, conv_w, conv_b, w_ffn_down, norm_ffn_post):
    p = dict(norm_mix_pre=norm_mix_pre, lam_q1=lam_q1, lam_k1=lam_k1, lam_q2=lam_q2, lam_k2=lam_k2,
             subln=subln, rpb=rpb, norm_mix_post=norm_mix_post, norm_ffn_pre=norm_ffn_pre,
             conv_w=conv_w, conv_b=conv_b, norm_ffn_post=norm_ffn_post)
    layers = [_layer_weights(l, w_in, w_br_a, w_br_b, w_out, w_ffn_up, w_ffn_down) for l in range(DEPTH)]
    y_prompt = _encode(x_prompt, meta_tokens, layers, p)
    y_sample = _encode(x_sample, meta_tokens, layers, p)
    return (y_prompt, y_sample)
```

```python
import functools
import math

import numpy as np
import jax
import jax.numpy as jnp
from jax import lax
from jax.experimental import pallas as pl
from jax.experimental.pallas import tpu as pltpu

D_MODEL = 2048
DEPTH = 2
N_META = 16
GRID_W = 64
H_A = 4
HD_A = 128
W_A = H_A * 2 * HD_A
H_B = 8
HD_B = 128
W_B = H_B * HD_B
NA_WIN_R = 8
NA_WIN_C = 16
D_FF = 5632
CONV_W = 3
ROPE_THETA = 10000.0
EPS = 1e-6

HEAD_ROWS = 128
PAD_ROWS = HEAD_ROWS - N_META
NEG_BIG = -1e30
LOG2_E = math.log2(math.e)
VMEM_LIMIT = 56 * 1024 * 1024

PROJ_W = 2 * W_A + 2 * W_B + 2 * D_MODEL
PROJ_TILE = 1024
N_PROJ_TILES = PROJ_W // PROJ_TILE
N_VT_TILES = (W_A + W_B) // PROJ_TILE

NA_GROUP_ROWS = 4
NA_KEY_ROWS = 12

F32 = jnp.float32
BF16 = jnp.bfloat16
NT_DIMS = (((1,), (1,)), ((), ()))


def _params(*sem):
    return pltpu.CompilerParams(dimension_semantics=sem, vmem_limit_bytes=VMEM_LIMIT)


def _rms(x, gain):
    return x * lax.rsqrt(jnp.mean(x * x, axis=-1, keepdims=True) + EPS) * gain


def _pad_row_mask(tile, tm, lp, batch):
    r = tile * tm + lax.broadcasted_iota(jnp.int32, (tm, 1), 0)
    m = r < PAD_ROWS
    for b in range(1, batch):
        m = m | ((r >= b * lp) & (r < b * lp + PAD_ROWS))
    return m


def _proj_kernel(x_ref, g_ref, cos_ref, sin_ref, w_ref, wvt_ref, proj_ref, vt_ref, hn_ref):
    j = pl.program_id(1)

    @pl.when(j == 0)
    def _():
        hn_ref[...] = _rms(x_ref[...], g_ref[...]).astype(BF16)

    def main_dot():
        return jnp.dot(hn_ref[...], w_ref[...], preferred_element_type=F32)

    def rope_store(acc, scale):
        c = cos_ref[...]
        s = sin_ref[...]
        for grp in range(PROJ_TILE // HD_A):
            xg = acc[:, grp * HD_A:(grp + 1) * HD_A]
            r = xg * c + pltpu.roll(xg, HD_A // 2, 1) * s
            if scale is not None:
                r = r * scale
            proj_ref[:, grp * HD_A:(grp + 1) * HD_A] = r.astype(BF16)

    @pl.when(j == 0)
    def _():
        rope_store(main_dot(), HD_A ** -0.5 * LOG2_E)

    @pl.when(j == 1)
    def _():
        rope_store(main_dot(), None)

    @pl.when(j == 2)
    def _():
        proj_ref[...] = (main_dot() * (HD_B ** -0.5 * LOG2_E)).astype(BF16)

    @pl.when(j == 3)
    def _():
        proj_ref[...] = main_dot().astype(BF16)

    @pl.when((j >= 4) & (j < N_PROJ_TILES))
    def _():
        proj_ref[...] = jax.nn.sigmoid(main_dot()).astype(BF16)

    @pl.when(j >= N_PROJ_TILES)
    def _():
        vt_ref[...] = lax.dot_general(wvt_ref[...], hn_ref[...], NT_DIMS,
                                      preferred_element_type=F32).astype(BF16)


def _proj_call(x, gain, cos_t, sin_t, w_main, w_vt, tm):
    m = x.shape[0]
    last = N_PROJ_TILES - 1
    return pl.pallas_call(
        _proj_kernel,
        grid=(m // tm, N_PROJ_TILES + N_VT_TILES),
        in_specs=[
            pl.BlockSpec((tm, D_MODEL), lambda i, j: (i, 0)),
            pl.BlockSpec((1, D_MODEL), lambda i, j: (0, 0)),
            pl.BlockSpec((tm, HD_A), lambda i, j: (i, 0)),
            pl.BlockSpec((tm, HD_A), lambda i, j: (i, 0)),
            pl.BlockSpec((D_MODEL, PROJ_TILE), lambda i, j: (0, jnp.minimum(j, last))),
            pl.BlockSpec((PROJ_TILE, D_MODEL), lambda i, j: (jnp.maximum(j - N_PROJ_TILES, 0), 0)),
        ],
        out_specs=[
            pl.BlockSpec((tm, PROJ_TILE), lambda i, j: (i, jnp.minimum(j, last))),
            pl.BlockSpec((PROJ_TILE, tm), lambda i, j: (jnp.maximum(j - N_PROJ_TILES, 0), i)),
        ],
        out_shape=[
            jax.ShapeDtypeStruct((m, PROJ_W), BF16),
            jax.ShapeDtypeStruct((W_A + W_B, m), BF16),
        ],
        scratch_shapes=[pltpu.VMEM((tm, D_MODEL), BF16)],
        compiler_params=_params("parallel", "arbitrary"),
        name="in_proj",
    )(x, gain, cos_t, sin_t, w_main, w_vt)


DA_TQ = 256
DA_TK = 512
DA_UNROLL = 4


def _dattn_kernel(q_ref, k_ref, vt_ref, lamv_ref, subln_ref, o_ref, acc0_ref, acc1_ref,
                  s00_ref, s01_ref, s10_ref, s11_ref, *, t_real, lam0):
    s_refs = ((s00_ref, s01_ref), (s10_ref, s11_ref))
    lv = lamv_ref[...]
    lam = (jnp.exp(jnp.sum(lv[0:1] * lv[1:2], axis=1, keepdims=True))
           - jnp.exp(jnp.sum(lv[2:3] * lv[3:4], axis=1, keepdims=True)) + lam0)
    subln = subln_ref[...]
    k_meta = k_ref[PAD_ROWS:HEAD_ROWS, :]
    vt_meta = vt_ref[:, PAD_ROWS:HEAD_ROWS]
    n_chunks = t_real // DA_TK
    acc_refs = (acc0_ref, acc1_ref)

    def q_tile(q0, tq):
        q = q_ref[pl.ds(q0, tq), :]
        qc = [q[:, c * HD_A:(c + 1) * HD_A] for c in range(2)]

        init = []
        for c in range(2):
            s = lax.dot_general(k_meta[:, c * HD_A:(c + 1) * HD_A], qc[c], NT_DIMS,
                                preferred_element_type=F32)
            m = jnp.max(s, axis=0, keepdims=True)
            p = jnp.exp2(s - m)
            acc_refs[c][:, 0:tq] = jnp.dot(vt_meta, p.astype(BF16), preferred_element_type=F32)
            init += [m, jnp.sum(p, axis=0, keepdims=True)]

        def chunk_start(j):
            return pl.multiple_of(HEAD_ROWS + j * DA_TK, 128)

        def scores(j, slot):
            kk = k_ref[pl.ds(chunk_start(j), DA_TK), :]
            for c in range(2):
                s_refs[slot][c][:, 0:tq] = lax.dot_general(
                    kk[:, c * HD_A:(c + 1) * HD_A], qc[c], NT_DIMS, preferred_element_type=F32)

        def accumulate(j, slot, carry):
            vt = vt_ref[:, pl.ds(chunk_start(j), DA_TK)]
            out = []
            for c in range(2):
                m_old, l_old = carry[2 * c], carry[2 * c + 1]
                s = s_refs[slot][c][:, 0:tq]
                m_new = jnp.maximum(m_old, jnp.max(s, axis=0, keepdims=True))
                alpha = jnp.exp2(m_old - m_new)
                p = jnp.exp2(s - m_new)
                acc_refs[c][:, 0:tq] = (alpha * acc_refs[c][:, 0:tq]
                                        + jnp.dot(vt, p.astype(BF16), preferred_element_type=F32))
                out += [m_new, alpha * l_old + jnp.sum(p, axis=0, keepdims=True)]
            return tuple(out)

        def chunk_group(jj, carry):
            j = DA_UNROLL * jj
            for u in range(DA_UNROLL):
                nxt = j + u + 1
                if u == DA_UNROLL - 1:
                    nxt = jnp.minimum(nxt, n_chunks - 1)
                scores(nxt, (u + 1) % 2)
                carry = accumulate(j + u, u % 2, carry)
            return carry

        scores(0, 0)
        _, l0, _, l1 = lax.fori_loop(0, n_chunks // DA_UNROLL, chunk_group, tuple(init))

        ot = acc0_ref[:, 0:tq] / l0 - lam * (acc1_ref[:, 0:tq] / l1)
        ms = jnp.mean(ot * ot, axis=0, keepdims=True)
        ot = ot * lax.rsqrt(ms + EPS) * subln * (1.0 - lam0)
        o_ref[pl.ds(q0, tq), :] = ot.T.astype(BF16)

    q_tile(0, HEAD_ROWS)

    def real_tile(i, carry):
        q_tile(pl.multiple_of(HEAD_ROWS + i * DA_TQ, 128), DA_TQ)
        return carry

    lax.fori_loop(0, t_real // DA_TQ, real_tile, 0)


def _dattn_call(proj3, vt, lamv, subln_col, layer, batch, lp):
    t_real = lp - HEAD_ROWS
    assert t_real % (DA_TK * DA_UNROLL) == 0 and t_real % DA_TQ == 0
    kern = functools.partial(_dattn_kernel, t_real=t_real,
                             lam0=0.8 - 0.6 * math.exp(-0.3 * layer))
    return pl.pallas_call(
        kern,
        grid=(batch, H_A),
        in_specs=[
            pl.BlockSpec((None, lp, 2 * HD_A), lambda b, h: (b, 0, h)),
            pl.BlockSpec((None, lp, 2 * HD_A), lambda b, h: (b, 0, H_A + h)),
            pl.BlockSpec((2 * HD_A, lp), lambda b, h: (h, b)),
            pl.BlockSpec((4, HD_A), lambda b, h: (0, 0)),
            pl.BlockSpec((2 * HD_A, 1), lambda b, h: (0, 0)),
        ],
        out_specs=pl.BlockSpec((None, lp, 2 * HD_A), lambda b, h: (b, 0, h)),
        out_shape=jax.ShapeDtypeStruct((batch, lp, W_A), BF16),
        scratch_shapes=[pltpu.VMEM((2 * HD_A, DA_TQ), F32)] * 2 + [pltpu.VMEM((DA_TK, DA_TQ), F32)] * 4,
        compiler_params=_params("parallel", "parallel"),
        name="diff_attn",
    )(proj3, proj3, vt, lamv, subln_col)


NA_TQ = NA_GROUP_ROWS * GRID_W
NA_TK = NA_KEY_ROWS * GRID_W


def _na_kernel(q_ref, k_ref, vt_ref, bias_ref, o_ref, *, rows):
    n_groups = rows // NA_GROUP_ROWS
    k_meta = k_ref[PAD_ROWS:HEAD_ROWS, :]
    vt_meta = vt_ref[:, PAD_ROWS:HEAD_ROWS]

    qm = q_ref[0:HEAD_ROWS, :]
    s = lax.dot_general(k_meta, qm, NT_DIMS, preferred_element_type=F32)
    p = jnp.exp2(s - jnp.max(s, axis=0, keepdims=True))
    ot = jnp.dot(vt_meta, p.astype(BF16), preferred_element_type=F32) / jnp.sum(p, axis=0, keepdims=True)
    o_ref[0:HEAD_ROWS, :] = ot.T.astype(BF16)

    def group(g, pattern, key_row):
        q0 = pl.multiple_of(HEAD_ROWS + g * NA_TQ, 128)
        k0 = pl.multiple_of(HEAD_ROWS + key_row * GRID_W, 128)
        q = q_ref[pl.ds(q0, NA_TQ), :]
        kw = k_ref[pl.ds(k0, NA_TK), :]
        vtw = vt_ref[:, pl.ds(k0, NA_TK)]
        s_win = lax.dot_general(kw, q, NT_DIMS, preferred_element_type=F32) + bias_ref[pattern]
        s_meta = lax.dot_general(k_meta, q, NT_DIMS, preferred_element_type=F32)
        m = jnp.maximum(jnp.max(s_win, axis=0, keepdims=True), jnp.max(s_meta, axis=0, keepdims=True))
        p_win = jnp.exp2(s_win - m)
        p_meta = jnp.exp2(s_meta - m)
        l = jnp.sum(p_win, axis=0, keepdims=True) + jnp.sum(p_meta, axis=0, keepdims=True)
        ot = (jnp.dot(vt_meta, p_meta.astype(BF16), preferred_element_type=F32)
              + jnp.dot(vtw, p_win.astype(BF16), preferred_element_type=F32)) / l
        o_ref[pl.ds(q0, NA_TQ), :] = ot.T.astype(BF16)

    group(0, 0, 0)

    def interior(g, carry):
        group(g, 1, g * NA_GROUP_ROWS - NA_WIN_R // 2)
        return carry

    lax.fori_loop(1, n_groups - 1, interior, 0)
    group(n_groups - 1, 2, rows - NA_KEY_ROWS)


def _na_call(proj3, vt, bias, batch, lp):
    rows = (lp - HEAD_ROWS) // GRID_W
    qb0 = (2 * W_A) // HD_B
    kb0 = (2 * W_A + W_B) // HD_B
    return pl.pallas_call(
        functools.partial(_na_kernel, rows=rows),
        grid=(H_B, batch),
        in_specs=[
            pl.BlockSpec((None, lp, HD_B), lambda h, b: (b, 0, qb0 + h)),
            pl.BlockSpec((None, lp, HD_B), lambda h, b: (b, 0, kb0 + h)),
            pl.BlockSpec((HD_B, lp), lambda h, b: (W_A // HD_B + h, b)),
            pl.BlockSpec((3, None, NA_TK, NA_TQ), lambda h, b: (0, h, 0, 0)),
        ],
        out_specs=pl.BlockSpec((None, lp, HD_B), lambda h, b: (b, 0, h)),
        out_shape=jax.ShapeDtypeStruct((batch, lp, W_B), BF16),
        compiler_params=_params("parallel", "parallel"),
        name="nbr_attn",
    )(proj3, proj3, vt, bias)


def _na_bias_tables(rpb_l, rows):
    wr = min(NA_WIN_R, rows)
    last_r0 = rows - NA_GROUP_ROWS
    cases = [(0, 0), (2 * NA_GROUP_ROWS, NA_GROUP_ROWS), (last_r0, rows - NA_KEY_ROWS)]
    n_dr = 2 * NA_WIN_R - 1
    n_dc = 2 * NA_WIN_C - 1
    toe = pl.pallas_call(
        _toeplitz_kernel,
        out_shape=jax.ShapeDtypeStruct((H_B * n_dr, GRID_W * GRID_W), F32),
        name="nbr_bias",
    )(rpb_l.reshape(H_B * n_dr, n_dc).astype(F32))
    toe = toe.reshape(H_B, n_dr, GRID_W, GRID_W)
    masked = jnp.full((H_B, GRID_W, GRID_W), NEG_BIG, F32)
    out = []
    for r0, kb in cases:
        slab = []
        for u in range(NA_KEY_ROWS):
            per_q = []
            for j in range(NA_GROUP_ROWS):
                r, kr = r0 + j, kb + u
                row_start = min(max(r - wr // 2, 0), rows - wr)
                inside = row_start <= kr < row_start + wr
                per_q.append(toe[:, kr - r + NA_WIN_R - 1] if inside else masked)
            slab.append(jnp.stack(per_q, axis=2))
        out.append(jnp.stack(slab, axis=1).reshape(H_B, NA_TK, NA_TQ))
    return jnp.stack(out, axis=0)


def _toeplitz_kernel(rpb_ref, o_ref):
    n = lax.broadcasted_iota(jnp.int32, (1, GRID_W * GRID_W), 1)
    kc = n >> (GRID_W.bit_length() - 1)
    c = n & (GRID_W - 1)
    dc = kc - c + (NA_WIN_C - 1)
    col_start = jnp.clip(c - NA_WIN_C // 2, 0, GRID_W - NA_WIN_C)
    inside = (kc >= col_start) & (kc < col_start + NA_WIN_C)
    acc = jnp.full(o_ref.shape, NEG_BIG, F32)
    for d in range(2 * NA_WIN_C - 1):
        acc = jnp.where(inside & (dc == d), rpb_ref[:, d:d + 1] * LOG2_E, acc)
    o_ref[...] = acc


def _mix_kernel(oa_ref, ob_ref, ga_ref, gb_ref, x_ref, wa_ref, wb_ref, wo_ref, gain_ref, o_ref,
                *, tm, lp, batch):
    a = jnp.dot(oa_ref[...], wa_ref[...], preferred_element_type=F32)
    b = jnp.dot(ob_ref[...], wb_ref[...], preferred_element_type=F32)
    mixed = (ga_ref[...].astype(F32) * a + gb_ref[...].astype(F32) * b).astype(BF16)
    y = jnp.dot(mixed, wo_ref[...], preferred_element_type=F32)
    xn = x_ref[...] + _rms(y, gain_ref[...])
    o_ref[...] = jnp.where(_pad_row_mask(pl.program_id(0), tm, lp, batch), 0.0, xn)


def _mix_call(o_a, o_b, proj, x, w_a, w_b, w_o, gain, tm, lp, batch):
    m = x.shape[0]
    ga_blk = (2 * W_A + 2 * W_B) // D_MODEL
    resident = functools.partial(pl.BlockSpec, pipeline_mode=pl.Buffered(1))
    return pl.pallas_call(
        functools.partial(_mix_kernel, tm=tm, lp=lp, batch=batch),
        grid=(m // tm,),
        in_specs=[
            pl.BlockSpec((tm, W_A), lambda i: (i, 0)),
            pl.BlockSpec((tm, W_B), lambda i: (i, 0)),
            pl.BlockSpec((tm, D_MODEL), lambda i: (i, ga_blk)),
            pl.BlockSpec((tm, D_MODEL), lambda i: (i, ga_blk + 1)),
            pl.BlockSpec((tm, D_MODEL), lambda i: (i, 0)),
            resident((W_A, D_MODEL), lambda i: (0, 0)),
            resident((W_B, D_MODEL), lambda i: (0, 0)),
            resident((D_MODEL, D_MODEL), lambda i: (0, 0)),
            pl.BlockSpec((1, D_MODEL), lambda i: (0, 0)),
        ],
        out_specs=pl.BlockSpec((tm, D_MODEL), lambda i: (i, 0)),
        out_shape=jax.ShapeDtypeStruct((m, D_MODEL), F32),
        compiler_params=_params("parallel"),
        name="mix",
    )(o_a, o_b, proj, proj, x, w_a, w_b, w_o, gain)


UP_TILE = 1024
DOWN_TK = 512
HALO = 16


def _up_kernel(x_ref, g_ref, w_ref, o_ref, hn_ref):
    @pl.when(pl.program_id(1) == 0)
    def _():
        hn_ref[...] = _rms(x_ref[...], g_ref[...]).astype(BF16)

    o_ref[...] = jnp.dot(hn_ref[...], w_ref[...], preferred_element_type=F32).astype(BF16)


def _up_call(x, gain, w_up, tm):
    m = x.shape[0]
    return pl.pallas_call(
        _up_kernel,
        grid=(m // tm, 2 * D_FF // UP_TILE),
        in_specs=[
            pl.BlockSpec((tm, D_MODEL), lambda i, j: (i, 0)),
            pl.BlockSpec((1, D_MODEL), lambda i, j: (0, 0)),
            pl.BlockSpec((D_MODEL, UP_TILE), lambda i, j: (0, j)),
        ],
        out_specs=pl.BlockSpec((tm, UP_TILE), lambda i, j: (i, j)),
        out_shape=jax.ShapeDtypeStruct((m, 2 * D_FF), BF16),
        scratch_shapes=[pltpu.VMEM((tm, D_MODEL), BF16)],
        compiler_params=_params("parallel", "arbitrary"),
        name="ffn_up",
    )(x, gain, w_up)


def _down_kernel(gate_ref, prev_ref, next_ref, val_ref, cw_ref, cb_ref, w_ref, x_ref, gain_ref,
                 o_ref, acc_ref, *, tm, n_tiles, lp, batch):
    i = pl.program_id(0)
    k = pl.program_id(1)
    g = gate_ref[...].astype(F32)
    prev_row = jnp.where(i == 0, 0.0, prev_ref[HALO - 1:HALO, :].astype(F32))
    next_row = jnp.where(i == n_tiles - 1, 0.0, next_ref[0:1, :].astype(F32))
    rid = lax.broadcasted_iota(jnp.int32, (tm, 1), 0)
    g_prev = jnp.where(rid == 0, prev_row, pltpu.roll(g, 1, 0))
    g_next = jnp.where(rid == tm - 1, next_row, pltpu.roll(g, tm - 1, 0))
    cw = cw_ref[...]
    a = cb_ref[...] + g_prev * cw[0:1] + g * cw[1:2] + g_next * cw[2:3]
    u = (jax.nn.gelu(a, approximate=True) * val_ref[...].astype(F32)).astype(BF16)
    contrib = jnp.dot(u, w_ref[...], preferred_element_type=F32)

    @pl.when(k == 0)
    def _():
        acc_ref[...] = contrib

    @pl.when(k > 0)
    def _():
        acc_ref[...] += contrib

    @pl.when(k == pl.num_programs(1) - 1)
    def _():
        xn = x_ref[...] + _rms(acc_ref[...], gain_ref[...])
        o_ref[...] = jnp.where(_pad_row_mask(i, tm, lp, batch), 0.0, xn)


def _down_call(up, x, cw, cb, w_down, gain, tm, lp, batch):
    m = x.shape[0]
    n_tiles = m // tm
    nk = D_FF // DOWN_TK
    hb = tm // HALO
    return pl.pallas_call(
        functools.partial(_down_kernel, tm=tm, n_tiles=n_tiles, lp=lp, batch=batch),
        grid=(n_tiles, nk),
        in_specs=[
            pl.BlockSpec((tm, DOWN_TK), lambda i, k: (i, k)),
            pl.BlockSpec((HALO, DOWN_TK), lambda i, k: (jnp.maximum(i * hb - 1, 0), k)),
            pl.BlockSpec((HALO, DOWN_TK), lambda i, k: (jnp.minimum((i + 1) * hb, m // HALO - 1), k)),
            pl.BlockSpec((tm, DOWN_TK), lambda i, k: (i, nk + k)),
            pl.BlockSpec((CONV_W, DOWN_TK), lambda i, k: (0, k)),
            pl.BlockSpec((1, DOWN_TK), lambda i, k: (0, k)),
            pl.BlockSpec((DOWN_TK, D_MODEL), lambda i, k: (k, 0)),
            pl.BlockSpec((tm, D_MODEL), lambda i, k: (i, 0)),
            pl.BlockSpec((1, D_MODEL), lambda i, k: (0, 0)),
        ],
        out_specs=pl.BlockSpec((tm, D_MODEL), lambda i, k: (i, 0)),
        out_shape=jax.ShapeDtypeStruct((m, D_MODEL), F32),
        scratch_shapes=[pltpu.VMEM((tm, D_MODEL), F32)],
        compiler_params=_params("parallel", "arbitrary"),
        name="ffn_down",
    )(up, up, up, up, cw, cb, w_down, x, gain)


def _rope_tables(lp, batch):
    pos = jnp.maximum(jnp.arange(lp, dtype=jnp.int32) - PAD_ROWS, 0).astype(F32)
    inv = 1.0 / (ROPE_THETA ** (jnp.arange(0, HD_A, 2, dtype=F32) / HD_A))
    ang = pos[:, None] * inv[None, :]
    cos, sin = jnp.cos(ang), jnp.sin(ang)
    cos_t = jnp.concatenate([cos, cos], axis=1)
    sin_t = jnp.concatenate([-sin, sin], axis=1)
    return jnp.tile(cos_t, (batch, 1)), jnp.tile(sin_t, (batch, 1))


def _layer_weights(l, w_in, w_br_a, w_br_b, w_out, w_ffn_up, w_ffn_down):
    wi = w_in[l]
    o = np.cumsum([0, W_A, W_A, W_A, W_B, W_B, W_B, D_MODEL, D_MODEL])
    seg = [wi[:, o[n]:o[n + 1]] for n in range(8)]
    w_main = jnp.concatenate([seg[0], seg[1], seg[3], seg[4], seg[6], seg[7]], axis=1).astype(BF16)
    w_vt = jnp.concatenate([seg[2], seg[5]], axis=1).T.astype(BF16)
    return dict(w_main=w_main, w_vt=w_vt, w_a=w_br_a[l].astype(BF16), w_b=w_br_b[l].astype(BF16),
                w_o=w_out[l].astype(BF16), w_up=w_ffn_up[l].astype(BF16),
                w_down=w_ffn_down[l].astype(BF16))


def _encode(x, meta_tokens, layers, p):
    batch, t_real, _ = x.shape
    lp = t_real + HEAD_ROWS
    m = batch * lp
    rows = t_real // GRID_W
    tm = 512
    tm_mix = 256
    tm_wide = m // 32 if m // 32 >= 1024 else m // 8
    head = jnp.concatenate([jnp.zeros((PAD_ROWS, D_MODEL), x.dtype), meta_tokens.astype(x.dtype)], axis=0)
    h = jnp.concatenate([jnp.broadcast_to(head[None], (batch, HEAD_ROWS, D_MODEL)), x], axis=1)
    h = h.reshape(m, D_MODEL)
    cos_t, sin_t = _rope_tables(lp, batch)
    for l in range(DEPTH):
        w = layers[l]
        row = lambda name: p[name][l].reshape(1, -1)
        proj, vt = _proj_call(h, row('norm_mix_pre'), cos_t, sin_t, w['w_main'], w['w_vt'], tm)
        proj3 = proj.reshape(batch, lp, PROJ_W)
        lamv = jnp.stack([p['lam_q1'][l], p['lam_k1'][l], p['lam_q2'][l], p['lam_k2'][l]]).astype(F32)
        o_a = _dattn_call(proj3, vt, lamv, p['subln'][l].reshape(-1, 1), l, batch, lp)
        o_b = _na_call(proj3, vt, _na_bias_tables(p['rpb'][l], rows), batch, lp)
        h = _mix_call(o_a.reshape(m, W_A), o_b.reshape(m, W_B), proj, h, w['w_a'], w['w_b'], w['w_o'],
                      row('norm_mix_post'), tm_mix, lp, batch)
        up = _up_call(h, row('norm_ffn_pre'), w['w_up'], tm_wide)
        h = _down_call(up, h, p['conv_w'][l], row('conv_b'), w['w_down'], row('norm_ffn_post'),
                       tm, lp, batch)
    return h.reshape(batch, lp, D_MODEL)[:, HEAD_ROWS:]


def kernel(x_prompt, x_sample, meta_tokens, norm_mix_pre, w_in, lam_q1, lam_k1, lam_q2, lam_k2, subln, rpb, w_br_a, w_br_b, w_out, norm_mix_post, norm_ffn_pre, w_ffn_up, conv_w, conv_b, w_ffn_down, norm_ffn_post):
    p = dict(norm_mix_pre=norm_mix_pre, lam_q1=lam_q1, lam_k1=lam_k1, lam_q2=lam_q2, lam_k2=lam_k2,
             subln=subln, rpb=rpb, norm_mix_post=norm_mix_post, norm_ffn_pre=norm_ffn_pre,
             conv_w=conv_w, conv_b=conv_b, norm_ffn_post=norm_ffn_post)
    layers = [_layer_weights(l, w_in, w_br_a, w_br_b, w_out, w_ffn_up, w_ffn_down) for l in range(DEPTH)]
    y_prompt = _encode(x_prompt, meta_tokens, layers, p)
    y_sample = _encode(x_sample, meta_tokens, layers, p)
    return (y_prompt, y_sample)
```

```python
import functools
import math

import numpy as np
import jax
import jax.numpy as jnp
from jax import lax
from jax.experimental import pallas as pl
from jax.experimental.pallas import tpu as pltpu

D_MODEL = 2048
DEPTH = 2
N_META = 16
GRID_W = 64
H_A = 4
HD_A = 128
W_A = H_A * 2 * HD_A
H_B = 8
HD_B = 128
W_B = H_B * HD_B
NA_WIN_R = 8
NA_WIN_C = 16
D_FF = 5632
CONV_W = 3
ROPE_THETA = 10000.0
EPS = 1e-6

HEAD_ROWS = 128
PAD_ROWS = HEAD_ROWS - N_META
NEG_BIG = -1e30
LOG2_E = math.log2(math.e)
VMEM_LIMIT = 56 * 1024 * 1024

PROJ_W = 2 * W_A + 2 * W_B + 2 * D_MODEL
PROJ_TILE = 1024
N_PROJ_TILES = PROJ_W // PROJ_TILE
N_VT_TILES = (W_A + W_B) // PROJ_TILE

NA_GROUP_ROWS = 4
NA_KEY_ROWS = 12

F32 = jnp.float32
BF16 = jnp.bfloat16
NT_DIMS = (((1,), (1,)), ((), ()))


def _params(*sem):
    return pltpu.CompilerParams(dimension_semantics=sem, vmem_limit_bytes=VMEM_LIMIT)


def _rms(x, gain):
    return x * lax.rsqrt(jnp.mean(x * x, axis=-1, keepdims=True) + EPS) * gain


def _pad_row_mask(tile, tm, lp, batch):
    r = tile * tm + lax.broadcasted_iota(jnp.int32, (tm, 1), 0)
    m = r < PAD_ROWS
    for b in range(1, batch):
        m = m | ((r >= b * lp) & (r < b * lp + PAD_ROWS))
    return m


def _proj_kernel(x_ref, g_ref, cos_ref, sin_ref, w_ref, wvt_ref, proj_ref, vt_ref, hn_ref):
    j = pl.program_id(1)

    @pl.when(j == 0)
    def _():
        hn_ref[...] = _rms(x_ref[...], g_ref[...]).astype(BF16)

    def main_dot():
        return jnp.dot(hn_ref[...], w_ref[...], preferred_element_type=F32)

    def rope_store(acc, scale):
        c = cos_ref[...]
        s = sin_ref[...]
        for grp in range(PROJ_TILE // HD_A):
            xg = acc[:, grp * HD_A:(grp + 1) * HD_A]
            r = xg * c + pltpu.roll(xg, HD_A // 2, 1) * s
            if scale is not None:
                r = r * scale
            proj_ref[:, grp * HD_A:(grp + 1) * HD_A] = r.astype(BF16)

    @pl.when(j == 0)
    def _():
        rope_store(main_dot(), HD_A ** -0.5 * LOG2_E)

    @pl.when(j == 1)
    def _():
        rope_store(main_dot(), None)

    @pl.when(j == 2)
    def _():
        proj_ref[...] = (main_dot() * (HD_B ** -0.5 * LOG2_E)).astype(BF16)

    @pl.when(j == 3)
    def _():
        proj_ref[...] = main_dot().astype(BF16)

    @pl.when((j >= 4) & (j < N_PROJ_TILES))
    def _():
        proj_ref[...] = jax.nn.sigmoid(main_dot()).astype(BF16)

    @pl.when(j >= N_PROJ_TILES)
    def _():
        vt_ref[...] = lax.dot_general(wvt_ref[...], hn_ref[...], NT_DIMS,
                                      preferred_element_type=F32).astype(BF16)


def _proj_call(x, gain, cos_t, sin_t, w_main, w_vt, tm):
    m = x.shape[0]
    last = N_PROJ_TILES - 1
    return pl.pallas_call(
        _proj_kernel,
        grid=(m // tm, N_PROJ_TILES + N_VT_TILES),
        in_specs=[
            pl.BlockSpec((tm, D_MODEL), lambda i, j: (i, 0)),
            pl.BlockSpec((1, D_MODEL), lambda i, j: (0, 0)),
            pl.BlockSpec((tm, HD_A), lambda i, j: (i, 0)),
            pl.BlockSpec((tm, HD_A), lambda i, j: (i, 0)),
            pl.BlockSpec((D_MODEL, PROJ_TILE), lambda i, j: (0, jnp.minimum(j, last))),
            pl.BlockSpec((PROJ_TILE, D_MODEL), lambda i, j: (jnp.maximum(j - N_PROJ_TILES, 0), 0)),
        ],
        out_specs=[
            pl.BlockSpec((tm, PROJ_TILE), lambda i, j: (i, jnp.minimum(j, last))),
            pl.BlockSpec((PROJ_TILE, tm), lambda i, j: (jnp.maximum(j - N_PROJ_TILES, 0), i)),
        ],
        out_shape=[
            jax.ShapeDtypeStruct((m, PROJ_W), BF16),
            jax.ShapeDtypeStruct((W_A + W_B, m), BF16),
        ],
        scratch_shapes=[pltpu.VMEM((tm, D_MODEL), BF16)],
        compiler_params=_params("parallel", "arbitrary"),
        name="in_proj",
    )(x, gain, cos_t, sin_t, w_main, w_vt)


DA_TQ = 256
DA_TK = 512
DA_UNROLL = 4


def _dattn_kernel(q_ref, k_ref, vt_ref, lamv_ref, subln_ref, o_ref, acc0_ref, acc1_ref,
                  s00_ref, s01_ref, s10_ref, s11_ref, *, t_real, lam0):
    s_refs = ((s00_ref, s01_ref), (s10_ref, s11_ref))
    lv = lamv_ref[...]
    lam = (jnp.exp(jnp.sum(lv[0:1] * lv[1:2], axis=1, keepdims=True))
           - jnp.exp(jnp.sum(lv[2:3] * lv[3:4], axis=1, keepdims=True)) + lam0)
    subln = subln_ref[...]
    k_meta = k_ref[PAD_ROWS:HEAD_ROWS, :]
    vt_meta = vt_ref[:, PAD_ROWS:HEAD_ROWS]
    n_chunks = t_real // DA_TK
    acc_refs = (acc0_ref, acc1_ref)

    def q_tile(q0, tq):
        q = q_ref[pl.ds(q0, tq), :]
        qc = [q[:, c * HD_A:(c + 1) * HD_A] for c in range(2)]

        init = []
        for c in range(2):
            s = lax.dot_general(k_meta[:, c * HD_A:(c + 1) * HD_A], qc[c], NT_DIMS,
                                preferred_element_type=F32)
            m = jnp.max(s, axis=0, keepdims=True)
            p = jnp.exp2(s - m)
            acc_refs[c][:, 0:tq] = jnp.dot(vt_meta, p.astype(BF16), preferred_element_type=F32)
            init += [m, jnp.sum(p, axis=0, keepdims=True)]

        def chunk_start(j):
            return pl.multiple_of(HEAD_ROWS + j * DA_TK, 128)

        def scores(j, slot):
            kk = k_ref[pl.ds(chunk_start(j), DA_TK), :]
            for c in range(2):
                s_refs[slot][c][:, 0:tq] = lax.dot_general(
                    kk[:, c * HD_A:(c + 1) * HD_A], qc[c], NT_DIMS, preferred_element_type=F32)

        def accumulate(j, slot, carry):
            vt = vt_ref[:, pl.ds(chunk_start(j), DA_TK)]
            out = []
            for c in range(2):
                m_old, l_old = carry[2 * c], carry[2 * c + 1]
                s = s_refs[slot][c][:, 0:tq]
                m_new = jnp.maximum(m_old, jnp.max(s, axis=0, keepdims=True))
                alpha = jnp.exp2(m_old - m_new)
                p = jnp.exp2(s - m_new)
                acc_refs[c][:, 0:tq] = (alpha * acc_refs[c][:, 0:tq]
                                        + jnp.dot(vt, p.astype(BF16), preferred_element_type=F32))
                out += [m_new, alpha * l_old + jnp.sum(p, axis=0, keepdims=True)]
            return tuple(out)

        def chunk_group(jj, carry):
            j = DA_UNROLL * jj
            for u in range(DA_UNROLL):
                nxt = j + u + 1
                if u == DA_UNROLL - 1:
                    nxt = jnp.minimum(nxt, n_chunks - 1)
                scores(nxt, (u + 1) % 2)
                carry = accumulate(j + u, u % 2, carry)
            return carry

        scores(0, 0)
        _, l0, _, l1 = lax.fori_loop(0, n_chunks // DA_UNROLL, chunk_group, tuple(init))

        ot = acc0_ref[:, 0:tq] / l0 - lam * (acc1_ref[:, 0:tq] / l1)
        ms = jnp.mean(ot * ot, axis=0, keepdims=True)
        ot = ot * lax.rsqrt(ms + EPS) * subln * (1.0 - lam0)
        o_ref[pl.ds(q0, tq), :] = ot.T.astype(BF16)

    q_tile(0, HEAD_ROWS)

    def real_tile(i, carry):
        q_tile(pl.multiple_of(HEAD_ROWS + i * DA_TQ, 128), DA_TQ)
        return carry

    lax.fori_loop(0, t_real // DA_TQ, real_tile, 0)


def _dattn_call(proj3, vt, lamv, subln_col, layer, batch, lp):
    t_real = lp - HEAD_ROWS
    assert t_real % (DA_TK * DA_UNROLL) == 0 and t_real % DA_TQ == 0
    kern = functools.partial(_dattn_kernel, t_real=t_real,
                             lam0=0.8 - 0.6 * math.exp(-0.3 * layer))
    return pl.pallas_call(
        kern,
        grid=(batch, H_A),
        in_specs=[
            pl.BlockSpec((None, lp, 2 * HD_A), lambda b, h: (b, 0, h)),
            pl.BlockSpec((None, lp, 2 * HD_A), lambda b, h: (b, 0, H_A + h)),
            pl.BlockSpec((2 * HD_A, lp), lambda b, h: (h, b)),
            pl.BlockSpec((4, HD_A), lambda b, h: (0, 0)),
            pl.BlockSpec((2 * HD_A, 1), lambda b, h: (0, 0)),
        ],
        out_specs=pl.BlockSpec((None, lp, 2 * HD_A), lambda b, h: (b, 0, h)),
        out_shape=jax.ShapeDtypeStruct((batch, lp, W_A), BF16),
        scratch_shapes=[pltpu.VMEM((2 * HD_A, DA_TQ), F32)] * 2 + [pltpu.VMEM((DA_TK, DA_TQ), F32)] * 4,
        compiler_params=_params("parallel", "parallel"),
        name="diff_attn",
    )(proj3, proj3, vt, lamv, subln_col)


NA_TQ = NA_GROUP_ROWS * GRID_W
NA_TK = NA_KEY_ROWS * GRID_W


def _na_kernel(q_ref, k_ref, vt_ref, bias_ref, o_ref, *, rows):
    n_groups = rows // NA_GROUP_ROWS
    k_meta = k_ref[PAD_ROWS:HEAD_ROWS, :]
    vt_meta = vt_ref[:, PAD_ROWS:HEAD_ROWS]

    qm = q_ref[0:HEAD_ROWS, :]
    s = lax.dot_general(k_meta, qm, NT_DIMS, preferred_element_type=F32)
    p = jnp.exp2(s - jnp.max(s, axis=0, keepdims=True))
    ot = jnp.dot(vt_meta, p.astype(BF16), preferred_element_type=F32) / jnp.sum(p, axis=0, keepdims=True)
    o_ref[0:HEAD_ROWS, :] = ot.T.astype(BF16)

    def group(g, pattern, key_row):
        q0 = pl.multiple_of(HEAD_ROWS + g * NA_TQ, 128)
        k0 = pl.multiple_of(HEAD_ROWS + key_row * GRID_W, 128)
        q = q_ref[pl.ds(q0, NA_TQ), :]
        kw = k_ref[pl.ds(k0, NA_TK), :]
        vtw = vt_ref[:, pl.ds(k0, NA_TK)]
        s_win = lax.dot_general(kw, q, NT_DIMS, preferred_element_type=F32) + bias_ref[pattern]
        s_meta = lax.dot_general(k_meta, q, NT_DIMS, preferred_element_type=F32)
        m = jnp.maximum(jnp.max(s_win, axis=0, keepdims=True), jnp.max(s_meta, axis=0, keepdims=True))
        p_win = jnp.exp2(s_win - m)
        p_meta = jnp.exp2(s_meta - m)
        l = jnp.sum(p_win, axis=0, keepdims=True) + jnp.sum(p_meta, axis=0, keepdims=True)
        ot = (jnp.dot(vt_meta, p_meta.astype(BF16), preferred_element_type=F32)
              + jnp.dot(vtw, p_win.astype(BF16), preferred_element_type=F32)) / l
        o_ref[pl.ds(q0, NA_TQ), :] = ot.T.astype(BF16)

    group(0, 0, 0)

    def interior(g, carry):
        group(g, 1, g * NA_GROUP_ROWS - NA_WIN_R // 2)
        return carry

    lax.fori_loop(1, n_groups - 1, interior, 0)
    group(n_groups - 1, 2, rows - NA_KEY_ROWS)


def _na_call(proj3, vt, bias, batch, lp):
    rows = (lp - HEAD_ROWS) // GRID_W
    qb0 = (2 * W_A) // HD_B
    kb0 = (2 * W_A + W_B) // HD_B
    return pl.pallas_call(
        functools.partial(_na_kernel, rows=rows),
        grid=(H_B, batch),
        in_specs=[
            pl.BlockSpec((None, lp, HD_B), lambda h, b: (b, 0, qb0 + h)),
            pl.BlockSpec((None, lp, HD_B), lambda h, b: (b, 0, kb0 + h)),
            pl.BlockSpec((HD_B, lp), lambda h, b: (W_A // HD_B + h, b)),
            pl.BlockSpec((3, None, NA_TK, NA_TQ), lambda h, b: (0, h, 0, 0)),
        ],
        out_specs=pl.BlockSpec((None, lp, HD_B), lambda h, b: (b, 0, h)),
        out_shape=jax.ShapeDtypeStruct((batch, lp, W_B), BF16),
        compiler_params=_params("parallel", "parallel"),
        name="nbr_attn",
    )(proj3, proj3, vt, bias)


def _na_bias_tables(rpb_l, rows):
    wr = min(NA_WIN_R, rows)
    last_r0 = rows - NA_GROUP_ROWS
    cases = [(0, 0), (2 * NA_GROUP_ROWS, NA_GROUP_ROWS), (last_r0, rows - NA_KEY_ROWS)]
    n_dr = 2 * NA_WIN_R - 1
    n_dc = 2 * NA_WIN_C - 1
    toe = pl.pallas_call(
        _toeplitz_kernel,
        out_shape=jax.ShapeDtypeStruct((H_B * n_dr, GRID_W * GRID_W), F32),
        name="nbr_bias",
    )(rpb_l.reshape(H_B * n_dr, n_dc).astype(F32))
    toe = toe.reshape(H_B, n_dr, GRID_W, GRID_W)
    masked = jnp.full((H_B, GRID_W, GRID_W), NEG_BIG, F32)
    out = []
    for r0, kb in cases:
        slab = []
        for u in range(NA_KEY_ROWS):
            per_q = []
            for j in range(NA_GROUP_ROWS):
                r, kr = r0 + j, kb + u
                row_start = min(max(r - wr // 2, 0), rows - wr)
                inside = row_start <= kr < row_start + wr
                per_q.append(toe[:, kr - r + NA_WIN_R - 1] if inside else masked)
            slab.append(jnp.stack(per_q, axis=2))
        out.append(jnp.stack(slab, axis=1).reshape(H_B, NA_TK, NA_TQ))
    return jnp.stack(out, axis=0)


def _toeplitz_kernel(rpb_ref, o_ref):
    n = lax.broadcasted_iota(jnp.int32, (1, GRID_W * GRID_W), 1)
    kc = n >> (GRID_W.bit_length() - 1)
    c = n & (GRID_W - 1)
    dc = kc - c + (NA_WIN_C - 1)
    col_start = jnp.clip(c - NA_WIN_C // 2, 0, GRID_W - NA_WIN_C)
    inside = (kc >= col_start) & (kc < col_start + NA_WIN_C)
    acc = jnp.full(o_ref.shape, NEG_BIG, F32)
    for d in range(2 * NA_WIN_C - 1):
        acc = jnp.where(inside & (dc == d), rpb_ref[:, d:d + 1] * LOG2_E, acc)
    o_ref[...] = acc


def _mix_kernel(oa_ref, ob_ref, ga_ref, gb_ref, x_ref, wa_ref, wb_ref, wo_ref, gain_ref, o_ref,
                *, tm, lp, batch):
    a = jnp.dot(oa_ref[...], wa_ref[...], preferred_element_type=F32)
    b = jnp.dot(ob_ref[...], wb_ref[...], preferred_element_type=F32)
    mixed = (ga_ref[...].astype(F32) * a + gb_ref[...].astype(F32) * b).astype(BF16)
    y = jnp.dot(mixed, wo_ref[...], preferred_element_type=F32)
    xn = x_ref[...] + _rms(y, gain_ref[...])
    o_ref[...] = jnp.where(_pad_row_mask(pl.program_id(0), tm, lp, batch), 0.0, xn)


def _mix_call(o_a, o_b, proj, x, w_a, w_b, w_o, gain, tm, lp, batch):
    m = x.shape[0]
    ga_blk = (2 * W_A + 2 * W_B) // D_MODEL
    resident = functools.partial(pl.BlockSpec, pipeline_mode=pl.Buffered(1))
    return pl.pallas_call(
        functools.partial(_mix_kernel, tm=tm, lp=lp, batch=batch),
        grid=(m // tm,),
        in_specs=[
            pl.BlockSpec((tm, W_A), lambda i: (i, 0)),
            pl.BlockSpec((tm, W_B), lambda i: (i, 0)),
            pl.BlockSpec((tm, D_MODEL), lambda i: (i, ga_blk)),
            pl.BlockSpec((tm, D_MODEL), lambda i: (i, ga_blk + 1)),
            pl.BlockSpec((tm, D_MODEL), lambda i: (i, 0)),
            resident((W_A, D_MODEL), lambda i: (0, 0)),
            resident((W_B, D_MODEL), lambda i: (0, 0)),
            resident((D_MODEL, D_MODEL), lambda i: (0, 0)),
            pl.BlockSpec((1, D_MODEL), lambda i: (0, 0)),
        ],
        out_specs=pl.BlockSpec((tm, D_MODEL), lambda i: (i, 0)),
        out_shape=jax.ShapeDtypeStruct((m, D_MODEL), F32),
        compiler_params=_params("parallel"),
        name="mix",
    )(o_a, o_b, proj, proj, x, w_a, w_b, w_o, gain)


UP_TILE = 1024
DOWN_TK = 512
HALO = 16


def _up_kernel(x_ref, g_ref, w_ref, o_ref, hn_ref):
    @pl.when(pl.program_id(1) == 0)
    def _():
        hn_ref[...] = _rms(x_ref[...], g_ref[...]).astype(BF16)

    o_ref[...] = jnp.dot(hn_ref[...], w_ref[...], preferred_element_type=F32).astype(BF16)


def _up_call(x, gain, w_up, tm):
    m = x.shape[0]
    return pl.pallas_call(
        _up_kernel,
        grid=(m // tm, 2 * D_FF // UP_TILE),
        in_specs=[
            pl.BlockSpec((tm, D_MODEL), lambda i, j: (i, 0)),
            pl.BlockSpec((1, D_MODEL), lambda i, j: (0, 0)),
            pl.BlockSpec((D_MODEL, UP_TILE), lambda i, j: (0, j)),
        ],
        out_specs=pl.BlockSpec((tm, UP_TILE), lambda i, j: (i, j)),
        out_shape=jax.ShapeDtypeStruct((m, 2 * D_FF), BF16),
        scratch_shapes=[pltpu.VMEM((tm, D_MODEL), BF16)],
        compiler_params=_params("parallel", "arbitrary"),
        name="ffn_up",
    )(x, gain, w_up)


def _down_kernel(gate_ref, prev_ref, next_ref, val_ref, cw_ref, cb_ref, w_ref, x_ref, gain_ref,
                 o_ref, acc_ref, *, tm, n_tiles, lp, batch):
    i = pl.program_id(0)
    k = pl.program_id(1)
    g = gate_ref[...].astype(F32)
    prev_row = jnp.where(i == 0, 0.0, prev_ref[HALO - 1:HALO, :].astype(F32))
    next_row = jnp.where(i == n_tiles - 1, 0.0, next_ref[0:1, :].astype(F32))
    rid = lax.broadcasted_iota(jnp.int32, (tm, 1), 0)
    g_prev = jnp.where(rid == 0, prev_row, pltpu.roll(g, 1, 0))
    g_next = jnp.where(rid == tm - 1, next_row, pltpu.roll(g, tm - 1, 0))
    cw = cw_ref[...]
    a = cb_ref[...] + g_prev * cw[0:1] + g * cw[1:2] + g_next * cw[2:3]
    u = (jax.nn.gelu(a, approximate=True) * val_ref[...].astype(F32)).astype(BF16)
    contrib = jnp.dot(u, w_ref[...], preferred_element_type=F32)

    @pl.when(k == 0)
    def _():
        acc_ref[...] = contrib

    @pl.when(k > 0)
    def _():
        acc_ref[...] += contrib

    @pl.when(k == pl.num_programs(1) - 1)
    def _():
        xn = x_ref[...] + _rms(acc_ref[...], gain_ref[...])
        o_ref[...] = jnp.where(_pad_row_mask(i, tm, lp, batch), 0.0, xn)


def _down_call(up, x, cw, cb, w_down, gain, tm, lp, batch):
    m = x.shape[0]
    n_tiles = m // tm
    nk = D_FF // DOWN_TK
    hb = tm // HALO
    return pl.pallas_call(
        functools.partial(_down_kernel, tm=tm, n_tiles=n_tiles, lp=lp, batch=batch),
        grid=(n_tiles, nk),
        in_specs=[
            pl.BlockSpec((tm, DOWN_TK), lambda i, k: (i, k)),
            pl.BlockSpec((HALO, DOWN_TK), lambda i, k: (jnp.maximum(i * hb - 1, 0), k)),
            pl.BlockSpec((HALO, DOWN_TK), lambda i, k: (jnp.minimum((i + 1) * hb, m // HALO - 1), k)),
            pl.BlockSpec((tm, DOWN_TK), lambda i, k: (i, nk + k)),
            pl.BlockSpec((CONV_W, DOWN_TK), lambda i, k: (0, k)),
            pl.BlockSpec((1, DOWN_TK), lambda i, k: (0, k)),
            pl.BlockSpec((DOWN_TK, D_MODEL), lambda i, k: (k, 0)),
            pl.BlockSpec((tm, D_MODEL), lambda i, k: (i, 0)),
            pl.BlockSpec((1, D_MODEL), lambda i, k: (0, 0)),
        ],
        out_specs=pl.BlockSpec((tm, D_MODEL), lambda i, k: (i, 0)),
        out_shape=jax.ShapeDtypeStruct((m, D_MODEL), F32),
        scratch_shapes=[pltpu.VMEM((tm, D_MODEL), F32)],
        compiler_params=_params("parallel", "arbitrary"),
        name="ffn_down",
    )(up, up, up, up, cw, cb, w_down, x, gain)


def _rope_tables(lp, batch):
    pos = jnp.maximum(jnp.arange(lp, dtype=jnp.int32) - PAD_ROWS, 0).astype(F32)
    inv = 1.0 / (ROPE_THETA ** (jnp.arange(0, HD_A, 2, dtype=F32) / HD_A))
    ang = pos[:, None] * inv[None, :]
    cos, sin = jnp.cos(ang), jnp.sin(ang)
    cos_t = jnp.concatenate([cos, cos], axis=1)
    sin_t = jnp.concatenate([-sin, sin], axis=1)
    return jnp.tile(cos_t, (batch, 1)), jnp.tile(sin_t, (batch, 1))


def _layer_weights(l, w_in, w_br_a, w_br_b, w_out, w_ffn_up, w_ffn_down):
    wi = w_in[l]
    o = np.cumsum([0, W_A, W_A, W_A, W_B, W_B, W_B, D_MODEL, D_MODEL])
    seg = [wi[:, o[n]:o[n + 1]] for n in range(8)]
    w_main = jnp.concatenate([seg[0], seg[1], seg[3], seg[4], seg[6], seg[7]], axis=1).astype(BF16)
    w_vt = jnp.concatenate([seg[2], seg[5]], axis=1).T.astype(BF16)
    return dict(w_main=w_main, w_vt=w_vt, w_a=w_br_a[l].astype(BF16), w_b=w_br_b[l].astype(BF16),
                w_o=w_out[l].astype(BF16), w_up=w_ffn_up[l].astype(BF16),
                w_down=w_ffn_down[l].astype(BF16))


def _encode(x, meta_tokens, layers, p):
    batch, t_real, _ = x.shape
    lp = t_real + HEAD_ROWS
    m = batch * lp
    rows = t_real // GRID_W
    tm = 512
    tm_mix = 256
    tm_wide = m // 32 if m // 32 >= 1024 else m // 8
    tm_proj = 640 if m % 640 == 0 else tm
    head = jnp.concatenate([jnp.zeros((PAD_ROWS, D_MODEL), x.dtype), meta_tokens.astype(x.dtype)], axis=0)
    h = jnp.concatenate([jnp.broadcast_to(head[None], (batch, HEAD_ROWS, D_MODEL)), x], axis=1)
    h = h.reshape(m, D_MODEL)
    cos_t, sin_t = _rope_tables(lp, batch)
    for l in range(DEPTH):
        w = layers[l]
        row = lambda name: p[name][l].reshape(1, -1)
        proj, vt = _proj_call(h, row('norm_mix_pre'), cos_t, sin_t, w['w_main'], w['w_vt'], tm_proj)
        proj3 = proj.reshape(batch, lp, PROJ_W)
        lamv = jnp.stack([p['lam_q1'][l], p['lam_k1'][l], p['lam_q2'][l], p['lam_k2'][l]]).astype(F32)
        o_a = _dattn_call(proj3, vt, lamv, p['subln'][l].reshape(-1, 1), l, batch, lp)
        o_b = _na_call(proj3, vt, _na_bias_tables(p['rpb'][l], rows), batch, lp)
        h = _mix_call(o_a.reshape(m, W_A), o_b.reshape(m, W_B), proj, h, w['w_a'], w['w_b'], w['w_o'],
                      row('norm_mix_post'), tm_mix, lp, batch)
        up = _up_call(h, row('norm_ffn_pre'), w['w_up'], tm_wide)
        h = _down_call(up, h, p['conv_w'][l], row('conv_b'), w['w_down'], row('norm_ffn_post'),
                       tm, lp, batch)
    return h.reshape(batch, lp, D_MODEL)[:, HEAD_ROWS:]


def kernel(x_prompt, x_sample, meta_tokens, norm_mix_pre, w_in, lam_q1, lam_k1, lam_q2, lam_k2, subln, rpb, w_br_a, w_br_b, w_out, norm_mix_post, norm_ffn_pre, w_ffn_up, conv_w, conv_b, w_ffn_down, norm_ffn_post):
    p = dict(norm_mix_pre=norm_mix_pre, lam_q1=lam_q1, lam_k1=lam_k1, lam_q2=lam_q2, lam_k2=lam_k2,
             subln=subln, rpb=rpb, norm_mix_post=norm_mix_post, norm_ffn_pre=norm_ffn_pre,
             conv_w=conv_w, conv_b=conv_b, norm_ffn_post=norm_ffn_post)
    layers = [_layer_weights(l, w_in, w_br_a, w_br_b, w_out, w_ffn_up, w_ffn_down) for l in range(DEPTH)]
    y_prompt = _encode(x_prompt, meta_tokens, layers, p)
    y_sample = _encode(x_sample, meta_tokens, layers, p)
    return (y_prompt, y_sample)
```

```python
import functools
import math

import numpy as np
import jax
import jax.numpy as jnp
from jax import lax
from jax.experimental import pallas as pl
from jax.experimental.pallas import tpu as pltpu

D_MODEL = 2048
DEPTH = 2
N_META = 16
GRID_W = 64
H_A = 4
HD_A = 128
W_A = H_A * 2 * HD_A
H_B = 8
HD_B = 128
W_B = H_B * HD_B
NA_WIN_R = 8
NA_WIN_C = 16
D_FF = 5632
CONV_W = 3
ROPE_THETA = 10000.0
EPS = 1e-6

HEAD_ROWS = 128
PAD_ROWS = HEAD_ROWS - N_META
NEG_BIG = -1e30
LOG2_E = math.log2(math.e)
VMEM_LIMIT = 56 * 1024 * 1024

PROJ_W = 2 * W_A + 2 * W_B + 2 * D_MODEL
PROJ_TILE = 1024
N_PROJ_TILES = PROJ_W // PROJ_TILE
N_VT_TILES = (W_A + W_B) // PROJ_TILE

NA_GROUP_ROWS = 4
NA_KEY_ROWS = 12

F32 = jnp.float32
BF16 = jnp.bfloat16
NT_DIMS = (((1,), (1,)), ((), ()))


def _params(*sem):
    return pltpu.CompilerParams(dimension_semantics=sem, vmem_limit_bytes=VMEM_LIMIT)


def _rms(x, gain):
    return x * lax.rsqrt(jnp.mean(x * x, axis=-1, keepdims=True) + EPS) * gain


def _pad_row_mask(tile, tm, lp, batch):
    r = tile * tm + lax.broadcasted_iota(jnp.int32, (tm, 1), 0)
    m = r < PAD_ROWS
    for b in range(1, batch):
        m = m | ((r >= b * lp) & (r < b * lp + PAD_ROWS))
    return m


def _proj_kernel(x_ref, g_ref, cos_ref, sin_ref, w_ref, wvt_ref, proj_ref, vt_ref, hn_ref):
    j = pl.program_id(1)

    @pl.when(j == 0)
    def _():
        hn_ref[...] = _rms(x_ref[...], g_ref[...]).astype(BF16)

    def main_dot():
        return jnp.dot(hn_ref[...], w_ref[...], preferred_element_type=F32)

    def rope_store(acc, scale):
        c = cos_ref[...]
        s = sin_ref[...]
        for grp in range(PROJ_TILE // HD_A):
            xg = acc[:, grp * HD_A:(grp + 1) * HD_A]
            r = xg * c + pltpu.roll(xg, HD_A // 2, 1) * s
            if scale is not None:
                r = r * scale
            proj_ref[:, grp * HD_A:(grp + 1) * HD_A] = r.astype(BF16)

    @pl.when(j == 0)
    def _():
        rope_store(main_dot(), HD_A ** -0.5 * LOG2_E)

    @pl.when(j == 1)
    def _():
        rope_store(main_dot(), None)

    @pl.when(j == 2)
    def _():
        proj_ref[...] = (main_dot() * (HD_B ** -0.5 * LOG2_E)).astype(BF16)

    @pl.when(j == 3)
    def _():
        proj_ref[...] = main_dot().astype(BF16)

    @pl.when((j >= 4) & (j < N_PROJ_TILES))
    def _():
        proj_ref[...] = jax.nn.sigmoid(main_dot()).astype(BF16)

    @pl.when(j >= N_PROJ_TILES)
    def _():
        vt_ref[...] = lax.dot_general(wvt_ref[...], hn_ref[...], NT_DIMS,
                                      preferred_element_type=F32).astype(BF16)


def _proj_call(x, gain, cos_t, sin_t, w_main, w_vt, tm):
    m = x.shape[0]
    last = N_PROJ_TILES - 1
    return pl.pallas_call(
        _proj_kernel,
        grid=(m // tm, N_PROJ_TILES + N_VT_TILES),
        in_specs=[
            pl.BlockSpec((tm, D_MODEL), lambda i, j: (i, 0)),
            pl.BlockSpec((1, D_MODEL), lambda i, j: (0, 0)),
            pl.BlockSpec((tm, HD_A), lambda i, j: (i, 0)),
            pl.BlockSpec((tm, HD_A), lambda i, j: (i, 0)),
            pl.BlockSpec((D_MODEL, PROJ_TILE), lambda i, j: (0, jnp.minimum(j, last))),
            pl.BlockSpec((PROJ_TILE, D_MODEL), lambda i, j: (jnp.maximum(j - N_PROJ_TILES, 0), 0)),
        ],
        out_specs=[
            pl.BlockSpec((tm, PROJ_TILE), lambda i, j: (i, jnp.minimum(j, last))),
            pl.BlockSpec((PROJ_TILE, tm), lambda i, j: (jnp.maximum(j - N_PROJ_TILES, 0), i)),
        ],
        out_shape=[
            jax.ShapeDtypeStruct((m, PROJ_W), BF16),
            jax.ShapeDtypeStruct((W_A + W_B, m), BF16),
        ],
        scratch_shapes=[pltpu.VMEM((tm, D_MODEL), BF16)],
        compiler_params=_params("parallel", "arbitrary"),
        name="in_proj",
    )(x, gain, cos_t, sin_t, w_main, w_vt)


DA_TQ = 256
DA_TK = 512
DA_UNROLL = 4


def _dattn_kernel(q_ref, k_ref, vt_ref, lamv_ref, subln_ref, o_ref, acc0_ref, acc1_ref,
                  s00_ref, s01_ref, s10_ref, s11_ref, *, t_real, lam0):
    s_refs = ((s00_ref, s01_ref), (s10_ref, s11_ref))
    lv = lamv_ref[...]
    lam = (jnp.exp(jnp.sum(lv[0:1] * lv[1:2], axis=1, keepdims=True))
           - jnp.exp(jnp.sum(lv[2:3] * lv[3:4], axis=1, keepdims=True)) + lam0)
    subln = subln_ref[...]
    k_meta = k_ref[PAD_ROWS:HEAD_ROWS, :]
    vt_meta = vt_ref[:, PAD_ROWS:HEAD_ROWS]
    n_chunks = t_real // DA_TK
    acc_refs = (acc0_ref, acc1_ref)

    def q_tile(q0, tq):
        q = q_ref[pl.ds(q0, tq), :]
        qc = [q[:, c * HD_A:(c + 1) * HD_A] for c in range(2)]

        init = []
        for c in range(2):
            s = lax.dot_general(k_meta[:, c * HD_A:(c + 1) * HD_A], qc[c], NT_DIMS,
                                preferred_element_type=F32)
            m = jnp.max(s, axis=0, keepdims=True)
            p = jnp.exp2(s - m)
            acc_refs[c][:, 0:tq] = jnp.dot(vt_meta, p.astype(BF16), preferred_element_type=F32)
            init += [m, jnp.sum(p, axis=0, keepdims=True)]

        def chunk_start(j):
            return pl.multiple_of(HEAD_ROWS + j * DA_TK, 128)

        def scores(j, slot):
            kk = k_ref[pl.ds(chunk_start(j), DA_TK), :]
            for c in range(2):
                s_refs[slot][c][:, 0:tq] = lax.dot_general(
                    kk[:, c * HD_A:(c + 1) * HD_A], qc[c], NT_DIMS, preferred_element_type=F32)

        def accumulate(j, slot, carry):
            vt = vt_ref[:, pl.ds(chunk_start(j), DA_TK)]
            out = []
            for c in range(2):
                m_old, l_old = carry[2 * c], carry[2 * c + 1]
                s = s_refs[slot][c][:, 0:tq]
                m_new = jnp.maximum(m_old, jnp.max(s, axis=0, keepdims=True))
                alpha = jnp.exp2(m_old - m_new)
                p = jnp.exp2(s - m_new)
                acc_refs[c][:, 0:tq] = (alpha * acc_refs[c][:, 0:tq]
                                        + jnp.dot(vt, p.astype(BF16), preferred_element_type=F32))
                out += [m_new, alpha * l_old + jnp.sum(p, axis=0, keepdims=True)]
            return tuple(out)

        def chunk_group(jj, carry):
            j = DA_UNROLL * jj
            for u in range(DA_UNROLL):
                nxt = j + u + 1
                if u == DA_UNROLL - 1:
                    nxt = jnp.minimum(nxt, n_chunks - 1)
                scores(nxt, (u + 1) % 2)
                carry = accumulate(j + u, u % 2, carry)
            return carry

        scores(0, 0)
        _, l0, _, l1 = lax.fori_loop(0, n_chunks // DA_UNROLL, chunk_group, tuple(init))

        ot = acc0_ref[:, 0:tq] / l0 - lam * (acc1_ref[:, 0:tq] / l1)
        ms = jnp.mean(ot * ot, axis=0, keepdims=True)
        ot = ot * lax.rsqrt(ms + EPS) * subln * (1.0 - lam0)
        o_ref[pl.ds(q0, tq), :] = ot.T.astype(BF16)

    q_tile(0, HEAD_ROWS)

    def real_tile(i, carry):
        q_tile(pl.multiple_of(HEAD_ROWS + i * DA_TQ, 128), DA_TQ)
        return carry

    lax.fori_loop(0, t_real // DA_TQ, real_tile, 0)


def _dattn_call(proj3, vt, lamv, subln_col, layer, batch, lp):
    t_real = lp - HEAD_ROWS
    assert t_real % (DA_TK * DA_UNROLL) == 0 and t_real % DA_TQ == 0
    kern = functools.partial(_dattn_kernel, t_real=t_real,
                             lam0=0.8 - 0.6 * math.exp(-0.3 * layer))
    return pl.pallas_call(
        kern,
        grid=(batch, H_A),
        in_specs=[
            pl.BlockSpec((None, lp, 2 * HD_A), lambda b, h: (b, 0, h)),
            pl.BlockSpec((None, lp, 2 * HD_A), lambda b, h: (b, 0, H_A + h)),
            pl.BlockSpec((2 * HD_A, lp), lambda b, h: (h, b)),
            pl.BlockSpec((4, HD_A), lambda b, h: (0, 0)),
            pl.BlockSpec((2 * HD_A, 1), lambda b, h: (0, 0)),
        ],
        out_specs=pl.BlockSpec((None, lp, 2 * HD_A), lambda b, h: (b, 0, h)),
        out_shape=jax.ShapeDtypeStruct((batch, lp, W_A), BF16),
        scratch_shapes=[pltpu.VMEM((2 * HD_A, DA_TQ), F32)] * 2 + [pltpu.VMEM((DA_TK, DA_TQ), F32)] * 4,
        compiler_params=_params("parallel", "parallel"),
        name="diff_attn",
    )(proj3, proj3, vt, lamv, subln_col)


NA_TQ = NA_GROUP_ROWS * GRID_W
NA_TK = NA_KEY_ROWS * GRID_W


def _na_kernel(q_ref, k_ref, vt_ref, bias_ref, o_ref, *, rows):
    n_groups = rows // NA_GROUP_ROWS
    k_meta = k_ref[PAD_ROWS:HEAD_ROWS, :]
    vt_meta = vt_ref[:, PAD_ROWS:HEAD_ROWS]

    qm = q_ref[0:HEAD_ROWS, :]
    s = lax.dot_general(k_meta, qm, NT_DIMS, preferred_element_type=F32)
    p = jnp.exp2(s - jnp.max(s, axis=0, keepdims=True))
    ot = jnp.dot(vt_meta, p.astype(BF16), preferred_element_type=F32) / jnp.sum(p, axis=0, keepdims=True)
    o_ref[0:HEAD_ROWS, :] = ot.T.astype(BF16)

    def group(g, pattern, key_row):
        q0 = pl.multiple_of(HEAD_ROWS + g * NA_TQ, 128)
        k0 = pl.multiple_of(HEAD_ROWS + key_row * GRID_W, 128)
        q = q_ref[pl.ds(q0, NA_TQ), :]
        kw = k_ref[pl.ds(k0, NA_TK), :]
        vtw = vt_ref[:, pl.ds(k0, NA_TK)]
        s_win = lax.dot_general(kw, q, NT_DIMS, preferred_element_type=F32) + bias_ref[pattern]
        s_meta = lax.dot_general(k_meta, q, NT_DIMS, preferred_element_type=F32)
        m = jnp.maximum(jnp.max(s_win, axis=0, keepdims=True), jnp.max(s_meta, axis=0, keepdims=True))
        p_win = jnp.exp2(s_win - m)
        p_meta = jnp.exp2(s_meta - m)
        l = jnp.sum(p_win, axis=0, keepdims=True) + jnp.sum(p_meta, axis=0, keepdims=True)
        ot = (jnp.dot(vt_meta, p_meta.astype(BF16), preferred_element_type=F32)
              + jnp.dot(vtw, p_win.astype(BF16), preferred_element_type=F32)) / l
        o_ref[pl.ds(q0, NA_TQ), :] = ot.T.astype(BF16)

    group(0, 0, 0)

    def interior(g, carry):
        group(g, 1, g * NA_GROUP_ROWS - NA_WIN_R // 2)
        return carry

    lax.fori_loop(1, n_groups - 1, interior, 0)
    group(n_groups - 1, 2, rows - NA_KEY_ROWS)


def _na_call(proj3, vt, bias, batch, lp):
    rows = (lp - HEAD_ROWS) // GRID_W
    qb0 = (2 * W_A) // HD_B
    kb0 = (2 * W_A + W_B) // HD_B
    return pl.pallas_call(
        functools.partial(_na_kernel, rows=rows),
        grid=(H_B, batch),
        in_specs=[
            pl.BlockSpec((None, lp, HD_B), lambda h, b: (b, 0, qb0 + h)),
            pl.BlockSpec((None, lp, HD_B), lambda h, b: (b, 0, kb0 + h)),
            pl.BlockSpec((HD_B, lp), lambda h, b: (W_A // HD_B + h, b)),
            pl.BlockSpec((3, None, NA_TK, NA_TQ), lambda h, b: (0, h, 0, 0)),
        ],
        out_specs=pl.BlockSpec((None, lp, HD_B), lambda h, b: (b, 0, h)),
        out_shape=jax.ShapeDtypeStruct((batch, lp, W_B), BF16),
        compiler_params=_params("parallel", "parallel"),
        name="nbr_attn",
    )(proj3, proj3, vt, bias)


def _na_bias_tables(rpb_l, rows):
    wr = min(NA_WIN_R, rows)
    last_r0 = rows - NA_GROUP_ROWS
    cases = [(0, 0), (2 * NA_GROUP_ROWS, NA_GROUP_ROWS), (last_r0, rows - NA_KEY_ROWS)]
    n_dr = 2 * NA_WIN_R - 1
    n_dc = 2 * NA_WIN_C - 1
    toe = pl.pallas_call(
        _toeplitz_kernel,
        out_shape=jax.ShapeDtypeStruct((H_B * n_dr, GRID_W * GRID_W), F32),
        name="nbr_bias",
    )(rpb_l.reshape(H_B * n_dr, n_dc).astype(F32))
    toe = toe.reshape(H_B, n_dr, GRID_W, GRID_W)
    masked = jnp.full((H_B, GRID_W, GRID_W), NEG_BIG, F32)
    out = []
    for r0, kb in cases:
        slab = []
        for u in range(NA_KEY_ROWS):
            per_q = []
            for j in range(NA_GROUP_ROWS):
                r, kr = r0 + j, kb + u
                row_start = min(max(r - wr // 2, 0), rows - wr)
                inside = row_start <= kr < row_start + wr
                per_q.append(toe[:, kr - r + NA_WIN_R - 1] if inside else masked)
            slab.append(jnp.stack(per_q, axis=2))
        out.append(jnp.stack(slab, axis=1).reshape(H_B, NA_TK, NA_TQ))
    return jnp.stack(out, axis=0)


def _toeplitz_kernel(rpb_ref, o_ref):
    n = lax.broadcasted_iota(jnp.int32, (1, GRID_W * GRID_W), 1)
    kc = n >> (GRID_W.bit_length() - 1)
    c = n & (GRID_W - 1)
    dc = kc - c + (NA_WIN_C - 1)
    col_start = jnp.clip(c - NA_WIN_C // 2, 0, GRID_W - NA_WIN_C)
    inside = (kc >= col_start) & (kc < col_start + NA_WIN_C)
    acc = jnp.full(o_ref.shape, NEG_BIG, F32)
    for d in range(2 * NA_WIN_C - 1):
        acc = jnp.where(inside & (dc == d), rpb_ref[:, d:d + 1] * LOG2_E, acc)
    o_ref[...] = acc


def _mix_kernel(oa_ref, ob_ref, ga_ref, gb_ref, x_ref, wa_ref, wb_ref, wo_ref, gain_ref, o_ref,
                *, tm, lp, batch):
    a = jnp.dot(oa_ref[...], wa_ref[...], preferred_element_type=F32)
    b = jnp.dot(ob_ref[...], wb_ref[...], preferred_element_type=F32)
    mixed = (ga_ref[...].astype(F32) * a + gb_ref[...].astype(F32) * b).astype(BF16)
    y = jnp.dot(mixed, wo_ref[...], preferred_element_type=F32)
    xn = x_ref[...] + _rms(y, gain_ref[...])
    o_ref[...] = jnp.where(_pad_row_mask(pl.program_id(0), tm, lp, batch), 0.0, xn)


def _mix_call(o_a, o_b, proj, x, w_a, w_b, w_o, gain, tm, lp, batch):
    m = x.shape[0]
    ga_blk = (2 * W_A + 2 * W_B) // D_MODEL
    resident = functools.partial(pl.BlockSpec, pipeline_mode=pl.Buffered(1))
    return pl.pallas_call(
        functools.partial(_mix_kernel, tm=tm, lp=lp, batch=batch),
        grid=(m // tm,),
        in_specs=[
            pl.BlockSpec((tm, W_A), lambda i: (i, 0)),
            pl.BlockSpec((tm, W_B), lambda i: (i, 0)),
            pl.BlockSpec((tm, D_MODEL), lambda i: (i, ga_blk)),
            pl.BlockSpec((tm, D_MODEL), lambda i: (i, ga_blk + 1)),
            pl.BlockSpec((tm, D_MODEL), lambda i: (i, 0)),
            resident((W_A, D_MODEL), lambda i: (0, 0)),
            resident((W_B, D_MODEL), lambda i: (0, 0)),
            resident((D_MODEL, D_MODEL), lambda i: (0, 0)),
            pl.BlockSpec((1, D_MODEL), lambda i: (0, 0)),
        ],
        out_specs=pl.BlockSpec((tm, D_MODEL), lambda i: (i, 0)),
        out_shape=jax.ShapeDtypeStruct((m, D_MODEL), F32),
        compiler_params=_params("parallel"),
        name="mix",
    )(o_a, o_b, proj, proj, x, w_a, w_b, w_o, gain)


UP_TILE = 1024
DOWN_TK = 512
DOWN_SUB = 128
HALO = 16


def _up_kernel(x_ref, g_ref, w_ref, o_ref, hn_ref):
    @pl.when(pl.program_id(1) == 0)
    def _():
        hn_ref[...] = _rms(x_ref[...], g_ref[...]).astype(BF16)

    o_ref[...] = jnp.dot(hn_ref[...], w_ref[...], preferred_element_type=F32).astype(BF16)


def _up_call(x, gain, w_up, tm):
    m = x.shape[0]
    return pl.pallas_call(
        _up_kernel,
        grid=(m // tm, 2 * D_FF // UP_TILE),
        in_specs=[
            pl.BlockSpec((tm, D_MODEL), lambda i, j: (i, 0)),
            pl.BlockSpec((1, D_MODEL), lambda i, j: (0, 0)),
            pl.BlockSpec((D_MODEL, UP_TILE), lambda i, j: (0, j)),
        ],
        out_specs=pl.BlockSpec((tm, UP_TILE), lambda i, j: (i, j)),
        out_shape=jax.ShapeDtypeStruct((m, 2 * D_FF), BF16),
        scratch_shapes=[pltpu.VMEM((tm, D_MODEL), BF16)],
        compiler_params=_params("parallel", "arbitrary"),
        name="ffn_up",
    )(x, gain, w_up)


def _down_kernel(gate_ref, prev_ref, next_ref, val_ref, cw_ref, cb_ref, w_ref, x_ref, gain_ref,
                 o_ref, acc_ref, *, tm, n_tiles, lp, batch):
    i = pl.program_id(0)
    k = pl.program_id(1)
    n_sub = tm // DOWN_SUB
    tile_prev = jnp.where(i == 0, 0.0, prev_ref[HALO - 1:HALO, :].astype(F32))
    tile_next = jnp.where(i == n_tiles - 1, 0.0, next_ref[0:1, :].astype(F32))
    rid = lax.broadcasted_iota(jnp.int32, (DOWN_SUB, 1), 0)
    cw = cw_ref[...]
    cb = cb_ref[...]

    @pl.when(k == 0)
    def _():
        acc_ref[...] = jnp.zeros_like(acc_ref)

    def sub_block(r, carry):
        r0 = pl.multiple_of(r * DOWN_SUB, DOWN_SUB)
        up0 = pl.multiple_of(jnp.maximum(r0 - HALO, 0), HALO)
        dn0 = pl.multiple_of(jnp.minimum(r0 + DOWN_SUB, tm - HALO), HALO)
        g = gate_ref[pl.ds(r0, DOWN_SUB), :].astype(F32)
        above = gate_ref[pl.ds(up0, HALO), :][HALO - 1:HALO].astype(F32)
        below = gate_ref[pl.ds(dn0, HALO), :][0:1].astype(F32)
        prev_row = jnp.where(r == 0, tile_prev, above)
        next_row = jnp.where(r == n_sub - 1, tile_next, below)
        g_prev = jnp.where(rid == 0, prev_row, pltpu.roll(g, 1, 0))
        g_next = jnp.where(rid == DOWN_SUB - 1, next_row, pltpu.roll(g, DOWN_SUB - 1, 0))
        a = cb + g_prev * cw[0:1] + g * cw[1:2] + g_next * cw[2:3]
        val = val_ref[pl.ds(r0, DOWN_SUB), :].astype(F32)
        u = (jax.nn.gelu(a, approximate=True) * val).astype(BF16)
        acc_ref[pl.ds(r0, DOWN_SUB), :] += jnp.dot(u, w_ref[...], preferred_element_type=F32)
        return carry

    lax.fori_loop(0, n_sub, sub_block, 0)

    @pl.when(k == pl.num_programs(1) - 1)
    def _():
        xn = x_ref[...] + _rms(acc_ref[...], gain_ref[...])
        o_ref[...] = jnp.where(_pad_row_mask(i, tm, lp, batch), 0.0, xn)


def _down_call(up, x, cw, cb, w_down, gain, tm, lp, batch):
    m = x.shape[0]
    n_tiles = m // tm
    nk = D_FF // DOWN_TK
    hb = tm // HALO
    return pl.pallas_call(
        functools.partial(_down_kernel, tm=tm, n_tiles=n_tiles, lp=lp, batch=batch),
        grid=(n_tiles, nk),
        in_specs=[
            pl.BlockSpec((tm, DOWN_TK), lambda i, k: (i, k)),
            pl.BlockSpec((HALO, DOWN_TK), lambda i, k: (jnp.maximum(i * hb - 1, 0), k)),
            pl.BlockSpec((HALO, DOWN_TK), lambda i, k: (jnp.minimum((i + 1) * hb, m // HALO - 1), k)),
            pl.BlockSpec((tm, DOWN_TK), lambda i, k: (i, nk + k)),
            pl.BlockSpec((CONV_W, DOWN_TK), lambda i, k: (0, k)),
            pl.BlockSpec((1, DOWN_TK), lambda i, k: (0, k)),
            pl.BlockSpec((DOWN_TK, D_MODEL), lambda i, k: (k, 0)),
            pl.BlockSpec((tm, D_MODEL), lambda i, k: (i, 0)),
            pl.BlockSpec((1, D_MODEL), lambda i, k: (0, 0)),
        ],
        out_specs=pl.BlockSpec((tm, D_MODEL), lambda i, k: (i, 0)),
        out_shape=jax.ShapeDtypeStruct((m, D_MODEL), F32),
        scratch_shapes=[pltpu.VMEM((tm, D_MODEL), F32)],
        compiler_params=_params("parallel", "arbitrary"),
        name="ffn_down",
    )(up, up, up, up, cw, cb, w_down, x, gain)


def _rope_tables(lp, batch):
    pos = jnp.maximum(jnp.arange(lp, dtype=jnp.int32) - PAD_ROWS, 0).astype(F32)
    inv = 1.0 / (ROPE_THETA ** (jnp.arange(0, HD_A, 2, dtype=F32) / HD_A))
    ang = pos[:, None] * inv[None, :]
    cos, sin = jnp.cos(ang), jnp.sin(ang)
    cos_t = jnp.concatenate([cos, cos], axis=1)
    sin_t = jnp.concatenate([-sin, sin], axis=1)
    return jnp.tile(cos_t, (batch, 1)), jnp.tile(sin_t, (batch, 1))


def _layer_weights(l, w_in, w_br_a, w_br_b, w_out, w_ffn_up, w_ffn_down):
    wi = w_in[l]
    o = np.cumsum([0, W_A, W_A, W_A, W_B, W_B, W_B, D_MODEL, D_MODEL])
    seg = [wi[:, o[n]:o[n + 1]] for n in range(8)]
    w_main = jnp.concatenate([seg[0], seg[1], seg[3], seg[4], seg[6], seg[7]], axis=1).astype(BF16)
    w_vt = jnp.concatenate([seg[2], seg[5]], axis=1).T.astype(BF16)
    return dict(w_main=w_main, w_vt=w_vt, w_a=w_br_a[l].astype(BF16), w_b=w_br_b[l].astype(BF16),
                w_o=w_out[l].astype(BF16), w_up=w_ffn_up[l].astype(BF16),
                w_down=w_ffn_down[l].astype(BF16))


def _encode(x, meta_tokens, layers, p):
    batch, t_real, _ = x.shape
    lp = t_real + HEAD_ROWS
    m = batch * lp
    rows = t_real // GRID_W
    tm = 512
    tm_mix = 256
    tm_wide = m // 32 if m // 32 >= 1024 else m // 8
    tm_proj = 640 if m % 640 == 0 else tm
    head = jnp.concatenate([jnp.zeros((PAD_ROWS, D_MODEL), x.dtype), meta_tokens.astype(x.dtype)], axis=0)
    h = jnp.concatenate([jnp.broadcast_to(head[None], (batch, HEAD_ROWS, D_MODEL)), x], axis=1)
    h = h.reshape(m, D_MODEL)
    cos_t, sin_t = _rope_tables(lp, batch)
    for l in range(DEPTH):
        w = layers[l]
        row = lambda name: p[name][l].reshape(1, -1)
        proj, vt = _proj_call(h, row('norm_mix_pre'), cos_t, sin_t, w['w_main'], w['w_vt'], tm_proj)
        proj3 = proj.reshape(batch, lp, PROJ_W)
        lamv = jnp.stack([p['lam_q1'][l], p['lam_k1'][l], p['lam_q2'][l], p['lam_k2'][l]]).astype(F32)
        o_a = _dattn_call(proj3, vt, lamv, p['subln'][l].reshape(-1, 1), l, batch, lp)
        o_b = _na_call(proj3, vt, _na_bias_tables(p['rpb'][l], rows), batch, lp)
        h = _mix_call(o_a.reshape(m, W_A), o_b.reshape(m, W_B), proj, h, w['w_a'], w['w_b'], w['w_o'],
                      row('norm_mix_post'), tm_mix, lp, batch)
        up = _up_call(h, row('norm_ffn_pre'), w['w_up'], tm_wide)
        h = _down_call(up, h, p['conv_w'][l], row('conv_b'), w['w_down'], row('norm_ffn_post'),
                       tm, lp, batch)
    return h.reshape(batch, lp, D_MODEL)[:, HEAD_ROWS:]


def kernel(x_prompt, x_sample, meta_tokens, norm_mix_pre, w_in, lam_q1, lam_k1, lam_q2, lam_k2, subln, rpb, w_br_a, w_br_b, w_out, norm_mix_post, norm_ffn_pre, w_ffn_up, conv_w, conv_b, w_ffn_down, norm_ffn_post):
    p = dict(norm_mix_pre=norm_mix_pre, lam_q1=lam_q1, lam_k1=lam_k1, lam_q2=lam_q2, lam_k2=lam_k2,
             subln=subln, rpb=rpb, norm_mix_post=norm_mix_post, norm_ffn_pre=norm_ffn_pre,
             conv_w=conv_w, conv_b=conv_b, norm_ffn_post=norm_ffn_post)
    layers = [_layer_weights(l, w_in, w_br_a, w_br_b, w_out, w_ffn_up, w_ffn_down) for l in range(DEPTH)]
    y_prompt = _encode(x_prompt, meta_tokens, layers, p)
    y_sample = _encode(x_sample, meta_tokens, layers, p)
    return (y_prompt, y_sample)
```
